```python
import jax, jax.numpy as jnp
from jax import lax
import numpy as np

D_MODEL = 1024
BATCH = 4
SEQ = 8192
DEPTH = 1

N_MEM = 256
CONV_CH = D_MODEL
CONV_WIDTH = 31
N_HEADS = 8
HEAD_DIM = D_MODEL // N_HEADS
ATTN_W = N_HEADS * HEAD_DIM
IDX_HEADS = 8
IDX_DIM = 64
TOPK_MAX = 256
Q_BLOCK = 128
X_HEADS = 4
X_HEAD_DIM = 128
X_W = X_HEADS * X_HEAD_DIM
D_FF = 4 * D_MODEL
EPS = 1e-6

IN_SIZES = (CONV_CH, CONV_CH, ATTN_W, ATTN_W, ATTN_W, IDX_HEADS * IDX_DIM, IDX_DIM, IDX_HEADS, D_MODEL, D_MODEL)
IN_WIDTH = sum(IN_SIZES)

kernel_name = "hybrid_conformer_dsa_gated_block"


def _rmsnorm(x, g):
    x32 = x.astype(jnp.float32)
    y = x32 * lax.rsqrt(jnp.mean(x32 * x32, axis=-1, keepdims=True) + EPS)
    return y.astype(x.dtype) * g


def _layernorm(x, g, b):
    x32 = x.astype(jnp.float32)
    mu = jnp.mean(x32, axis=-1, keepdims=True)
    xc = x32 - mu
    var = jnp.mean(xc * xc, axis=-1, keepdims=True)
    return (xc * lax.rsqrt(var + EPS)).astype(x.dtype) * g + b


def _split_cols(z, sizes):
    offs = []
    acc = 0
    for s in sizes[:-1]:
        acc += s
        offs.append(acc)
    return jnp.split(z, offs, axis=-1)


def _conformer_conv(a, gate, conv_w, conv_b, ln_g, ln_b, w_out):
    u = a * jax.nn.sigmoid(gate)
    u = lax.conv_general_dilated(
        u, conv_w[:, None, :].astype(u.dtype), window_strides=(1,),
        padding=[(CONV_WIDTH - 1, 0)], dimension_numbers=('NWC', 'WIO', 'NWC'),
        feature_group_count=CONV_CH) + conv_b
    u = jax.nn.silu(_layernorm(u, ln_g, ln_b))
    return u @ w_out


def _dsa_attention(q, k, v, qi, ki, wi):
    B, L = q.shape[0], q.shape[1]
    k_top = min(TOPK_MAX, L // 4)
    nb = L // Q_BLOCK
    key_pos = jnp.arange(L)
    ki32 = ki.astype(jnp.float32)

    def to_blocks(a):
        return a.reshape((B, nb, Q_BLOCK) + a.shape[2:]).swapaxes(0, 1)

    def block(args):
        qb, qib, wb, bi = args
        q_pos = bi * Q_BLOCK + jnp.arange(Q_BLOCK)
        causal = key_pos[None, :] <= q_pos[:, None]
        logits = jnp.einsum('bqhd,bsd->bqhs', qib.astype(jnp.float32), ki32) * (IDX_DIM ** -0.5)
        score = jnp.einsum('bqh,bqhs->bqs', wb.astype(jnp.float32), jax.nn.relu(logits))
        score = jnp.where(causal[None], score, -jnp.inf)
        _, idx = lax.top_k(score, k_top)
        valid = idx <= q_pos[None, :, None]
        k_sel = jax.vmap(lambda kb, ib: kb[ib])(k, idx)
        v_sel = jax.vmap(lambda vb, ib: vb[ib])(v, idx)
        s = jnp.einsum('bqhd,bqkhd->bhqk', qb.astype(jnp.float32), k_sel.astype(jnp.float32)) * (HEAD_DIM ** -0.5)
        s = jnp.where(valid[:, None], s, -jnp.inf)
        p = jax.nn.softmax(s, axis=-1)
        return jnp.einsum('bhqk,bqkhd->bqhd', p.astype(v.dtype), v_sel)

    out = lax.map(block, (to_blocks(q), to_blocks(qi), to_blocks(wi), jnp.arange(nb)))
    return out.swapaxes(0, 1).reshape(B, L, ATTN_W)


def _cross_attention(hn, memn, wq, wkv, wo):
    B, L, _ = hn.shape
    q = (hn @ wq).reshape(B, L, X_HEADS, X_HEAD_DIM)
    kv = memn @ wkv
    k, v = jnp.split(kv, 2, axis=-1)
    k = k.reshape(B, N_MEM, X_HEADS, X_HEAD_DIM)
    v = v.reshape(B, N_MEM, X_HEADS, X_HEAD_DIM)
    s = jnp.einsum('bshd,bmhd->bhsm', q.astype(jnp.float32), k.astype(jnp.float32)) * (X_HEAD_DIM ** -0.5)
    p = jax.nn.softmax(s, axis=-1)
    o = jnp.einsum('bhsm,bmhd->bshd', p.astype(v.dtype), v).reshape(B, L, X_W)
    return o @ wo


def setup_inputs(seed: int = 0) -> dict:
    key = jax.random.key(seed)
    ks = jax.random.split(key, 24)
    f32 = jnp.float32

    def w(k, shape, fan_in):
        return jax.random.normal(k, shape, f32) * (fan_in ** -0.5)

    def gain(k, shape):
        return 1.0 + 0.02 * jax.random.normal(k, shape, f32)

    def bias(k, shape):
        return 0.02 * jax.random.normal(k, shape, f32)

    return {
        "x": jax.random.normal(ks[0], (BATCH, SEQ, D_MODEL), f32),
        "mem": jax.random.normal(ks[1], (BATCH, N_MEM, D_MODEL), f32),
        "norm_mix_g": gain(ks[2], (DEPTH, D_MODEL)),
        "w_in": w(ks[3], (DEPTH, D_MODEL, IN_WIDTH), D_MODEL),
        "b_gate": bias(ks[4], (DEPTH, 2 * D_MODEL)),
        "conv_w": w(ks[5], (DEPTH, CONV_WIDTH, CONV_CH), CONV_WIDTH),
        "conv_b": bias(ks[6], (DEPTH, CONV_CH)),
        "conv_ln_g": gain(ks[7], (DEPTH, CONV_CH)),
        "conv_ln_b": bias(ks[8], (DEPTH, CONV_CH)),
        "w_conv_out": w(ks[9], (DEPTH, CONV_CH, D_MODEL), CONV_CH),
        "w_attn_out": w(ks[10], (DEPTH, ATTN_W, D_MODEL), ATTN_W),
        "w_mix_out": w(ks[11], (DEPTH, D_MODEL, D_MODEL), D_MODEL),
        "norm_x_g": gain(ks[12], (DEPTH, D_MODEL)),
        "norm_mem_g": gain(ks[13], (DEPTH, D_MODEL)),
        "wx_q": w(ks[14], (DEPTH, D_MODEL, X_W), D_MODEL),
        "wx_kv": w(ks[15], (DEPTH, D_MODEL, 2 * X_W), D_MODEL),
        "wx_o": w(ks[16], (DEPTH, X_W, D_MODEL), X_W),
        "norm_ffn_g": gain(ks[17], (DEPTH, D_MODEL)),
        "w_ff1": w(ks[18], (DEPTH, D_MODEL, D_FF), D_MODEL),
        "w_ff2": w(ks[19], (DEPTH, D_FF, D_MODEL), D_FF),
        "norm_final_g": gain(ks[20], (D_MODEL,)),
    }


def reference(x, mem, norm_mix_g, w_in, b_gate, conv_w, conv_b, conv_ln_g, conv_ln_b,
              w_conv_out, w_attn_out, w_mix_out, norm_x_g, norm_mem_g, wx_q, wx_kv, wx_o,
              norm_ffn_g, w_ff1, w_ff2, norm_final_g):
    B, L, _ = x.shape
    h = x
    for l in range(DEPTH):
        n = _rmsnorm(h, norm_mix_g[l])
        z = n @ w_in[l]
        c_a, c_g, q, k, v, qi, ki, wi, gc_logit, ga_logit = _split_cols(z, IN_SIZES)
        bg_conv, bg_attn = jnp.split(b_gate[l], 2)
        g_conv = jax.nn.sigmoid(gc_logit + bg_conv)
        g_attn = jax.nn.sigmoid(ga_logit + bg_attn)
        y_conv = _conformer_conv(c_a, c_g, conv_w[l], conv_b[l], conv_ln_g[l], conv_ln_b[l], w_conv_out[l])
        qh = q.reshape(B, L, N_HEADS, HEAD_DIM)
        kh = k.reshape(B, L, N_HEADS, HEAD_DIM)
        vh = v.reshape(B, L, N_HEADS, HEAD_DIM)
        qih = qi.reshape(B, L, IDX_HEADS, IDX_DIM)
        wih = wi * (IDX_HEADS ** -0.5)
        y_attn = _dsa_attention(qh, kh, vh, qih, ki, wih) @ w_attn_out[l]
        h = h + (g_conv * y_conv + g_attn * y_attn) @ w_mix_out[l]
        h = h + _cross_attention(_rmsnorm(h, norm_x_g[l]), _rmsnorm(mem, norm_mem_g[l]),
                                 wx_q[l], wx_kv[l], wx_o[l])
        hf = _rmsnorm(h, norm_ffn_g[l]) @ w_ff1[l]
        h = h + jnp.square(jax.nn.relu(hf)) @ w_ff2[l]
    return _rmsnorm(h, norm_final_g)
```

```python
import functools

import numpy as np
import jax
import jax.numpy as jnp
from jax import lax
from jax.experimental import pallas as pl
from jax.experimental.pallas import tpu as pltpu

F32 = jnp.float32
BF16 = jnp.bfloat16
I32 = jnp.int32

EPS = 1e-6
D_MODEL = 1024
N_HEADS = 8
HEAD_DIM = 128
IDX_HEADS = 8
IDX_DIM = 64
TOPK_MAX = 256
CONV_WIDTH = 31
X_HEADS = 4
X_HEAD_DIM = 128
X_W = X_HEADS * X_HEAD_DIM
LANES = 128

INT_MIN = -(2 ** 31)
KEY_NEG_INF = int(np.int32(np.uint32(0xFF800000) ^ np.uint32(0x7FFFFFFF)))

VMEM_LIMIT = 56 * 1024 * 1024

_NT = (((1,), (1,)), ((), ()))


def _params(sem):
    return pltpu.CompilerParams(dimension_semantics=sem, vmem_limit_bytes=VMEM_LIMIT)


def _rms_bf16(x, g):
    ms = jnp.mean(x * x, axis=-1, keepdims=True)
    return (x * lax.rsqrt(ms + EPS) * g).astype(BF16)


def _sigmoid(x):
    return 1.0 / (1.0 + jnp.exp(-x))


def _proj_u_kernel(x_ref, g_ref, wa_ref, wg_ref, o_ref, xn_ref):
    @pl.when(pl.program_id(1) == 0)
    def _():
        xn_ref[...] = _rms_bf16(x_ref[...], g_ref[...])

    xn = xn_ref[...]
    a = jnp.dot(xn, wa_ref[...], preferred_element_type=F32)
    gt = jnp.dot(xn, wg_ref[...], preferred_element_type=F32)
    o_ref[...] = (a * _sigmoid(gt)).astype(BF16)


def _proj_qkv_kernel(x_ref, g_ref, w_ref, o_ref, xn_ref):
    j = pl.program_id(1)

    @pl.when(j == 0)
    def _():
        xn_ref[...] = _rms_bf16(x_ref[...], g_ref[...])

    acc = jnp.dot(xn_ref[...], w_ref[...], preferred_element_type=F32)
    scale = jnp.where(j == 0, HEAD_DIM ** -0.5, 1.0).astype(F32)
    o_ref[...] = (acc * scale).astype(BF16)


def _proj_idx_kernel(x_ref, g_ref, wqi_ref, wkw_ref, qi_ref, ki_ref, kw_ref):
    xn = _rms_bf16(x_ref[...], g_ref[...])
    qi = jnp.dot(xn, wqi_ref[...], preferred_element_type=F32)
    qi_ref[...] = (qi * (IDX_DIM ** -0.5)).astype(BF16)
    kw = jnp.dot(xn, wkw_ref[...], preferred_element_type=F32)
    ki_ref[...] = kw[:, :IDX_DIM].astype(BF16)
    kw_ref[...] = kw * (IDX_HEADS ** -0.5)


def _proj_gate_kernel(x_ref, g_ref, w_ref, b_ref, o_ref, xn_ref):
    @pl.when(pl.program_id(1) == 0)
    def _():
        xn_ref[...] = _rms_bf16(x_ref[...], g_ref[...])

    acc = jnp.dot(xn_ref[...], w_ref[...], preferred_element_type=F32)
    o_ref[...] = _sigmoid(acc + b_ref[...]).astype(BF16)


def _projections(x2, g, w_a, w_g, w_qkv, w_qi, w_kw, w_gate, b_gate, tm):
    T, D = x2.shape
    nt = T // tm
    x_spec2 = pl.BlockSpec((tm, D), lambda i, j: (i, 0))
    g_spec2 = pl.BlockSpec((1, D), lambda i, j: (0, 0))

    tn_u = 512
    u = pl.pallas_call(
        _proj_u_kernel,
        grid=(nt, D // tn_u),
        in_specs=[x_spec2, g_spec2,
                  pl.BlockSpec((D, tn_u), lambda i, j: (0, j)),
                  pl.BlockSpec((D, tn_u), lambda i, j: (0, j))],
        out_specs=pl.BlockSpec((tm, tn_u), lambda i, j: (i, j)),
        out_shape=jax.ShapeDtypeStruct((T, D), BF16),
        scratch_shapes=[pltpu.VMEM((tm, D), BF16)],
        compiler_params=_params(("parallel", "arbitrary")),
        name="proj_u",
    )(x2, g, w_a, w_g)

    qkv = pl.pallas_call(
        _proj_qkv_kernel,
        grid=(nt, 3),
        in_specs=[x_spec2, g_spec2, pl.BlockSpec((D, D), lambda i, j: (0, j))],
        out_specs=pl.BlockSpec((tm, D), lambda i, j: (i, j)),
        out_shape=jax.ShapeDtypeStruct((T, 3 * D), BF16),
        scratch_shapes=[pltpu.VMEM((tm, D), BF16)],
        compiler_params=_params(("parallel", "arbitrary")),
        name="proj_qkv",
    )(x2, g, w_qkv)

    nqi = IDX_HEADS * IDX_DIM
    qi, ki, kw = pl.pallas_call(
        _proj_idx_kernel,
        grid=(nt,),
        in_specs=[pl.BlockSpec((tm, D), lambda i: (i, 0)),
                  pl.BlockSpec((1, D), lambda i: (0, 0)),
                  pl.BlockSpec((D, nqi), lambda i: (0, 0)),
                  pl.BlockSpec((D, LANES), lambda i: (0, 0))],
        out_specs=[pl.BlockSpec((tm, nqi), lambda i: (i, 0)),
                   pl.BlockSpec((tm, IDX_DIM), lambda i: (i, 0)),
                   pl.BlockSpec((tm, LANES), lambda i: (i, 0))],
        out_shape=[jax.ShapeDtypeStruct((T, nqi), BF16),
                   jax.ShapeDtypeStruct((T, IDX_DIM), BF16),
                   jax.ShapeDtypeStruct((T, LANES), F32)],
        compiler_params=_params(("parallel",)),
        name="proj_idx",
    )(x2, g, w_qi, w_kw)

    gates = pl.pallas_call(
        _proj_gate_kernel,
        grid=(nt, 2),
        in_specs=[x_spec2, g_spec2,
                  pl.BlockSpec((D, D), lambda i, j: (0, j)),
                  pl.BlockSpec((1, D), lambda i, j: (0, j))],
        out_specs=pl.BlockSpec((tm, D), lambda i, j: (i, j)),
        out_shape=jax.ShapeDtypeStruct((T, 2 * D), BF16),
        scratch_shapes=[pltpu.VMEM((tm, D), BF16)],
        compiler_params=_params(("parallel", "arbitrary")),
        name="proj_gate",
    )(x2, g, w_gate, b_gate)
    return u, qkv, qi, ki, kw, gates


CONV_HALO = 32
CONV_ROWS = 32
CONV_LANES = 256


def _conv_kernel(u_ref, halo_ref, cw_ref, cb_ref, lng_ref, lnb_ref, wout_ref, gc_ref,
                 o_ref, win_ref, cv_ref):
    tl = u_ref.shape[1]
    C = u_ref.shape[2]
    i = pl.program_id(1)
    halo = halo_ref[0].astype(F32)
    win_ref[0:CONV_HALO, :] = jnp.where(i > 0, halo, 0.0)
    win_ref[CONV_HALO:CONV_HALO + tl, :] = u_ref[0].astype(F32)
    win_ref[CONV_HALO + tl:, :] = jnp.zeros((8, C), F32)

    s0 = CONV_HALO - (CONV_WIDTH - 1)
    for lc in range(C // CONV_LANES):
        lanes = slice(lc * CONV_LANES, (lc + 1) * CONV_LANES)

        def body(r, carry, lanes=lanes):
            r0 = pl.multiple_of(r * CONV_ROWS, CONV_ROWS)
            out = jnp.zeros((CONV_ROWS, CONV_LANES), F32)
            for res in range(8):
                part = None
                for j in range(CONV_WIDTH):
                    s = j + s0
                    if s % 8 != res:
                        continue
                    a = s - res
                    term = cw_ref[j:j + 1, lanes] * win_ref[pl.ds(r0 + a, CONV_ROWS + 8), lanes]
                    part = term if part is None else part + term
                if part is not None:
                    out = out + part[res:res + CONV_ROWS, :]
            cv_ref[pl.ds(r0, CONV_ROWS), lanes] = out + cb_ref[:, lanes]
            return carry

        lax.fori_loop(0, tl // CONV_ROWS, body, 0)

    y = cv_ref[...]
    mu = jnp.mean(y, axis=-1, keepdims=True)
    yc = y - mu
    var = jnp.mean(yc * yc, axis=-1, keepdims=True)
    z = yc * lax.rsqrt(var + EPS) * lng_ref[...] + lnb_ref[...]
    act = (z * _sigmoid(z)).astype(BF16)
    out = jnp.dot(act, wout_ref[...], preferred_element_type=F32)
    o_ref[0] = (gc_ref[0].astype(F32) * out).astype(BF16)


def _conv_branch(u3, gates3, conv_w, conv_b, ln_g, ln_b, w_out, tl):
    B, L, C = u3.shape
    hb = tl // CONV_HALO
    return pl.pallas_call(
        _conv_kernel,
        grid=(B, L // tl),
        in_specs=[pl.BlockSpec((1, tl, C), lambda b, i: (b, i, 0)),
                  pl.BlockSpec((1, CONV_HALO, C), lambda b, i: (b, jnp.maximum(i * hb - 1, 0), 0)),
                  pl.BlockSpec((CONV_WIDTH, C), lambda b, i: (0, 0)),
                  pl.BlockSpec((1, C), lambda b, i: (0, 0)),
                  pl.BlockSpec((1, C), lambda b, i: (0, 0)),
                  pl.BlockSpec((1, C), lambda b, i: (0, 0)),
                  pl.BlockSpec((C, C), lambda b, i: (0, 0)),
                  pl.BlockSpec((1, tl, C), lambda b, i: (b, i, 0))],
        out_specs=pl.BlockSpec((1, tl, C), lambda b, i: (b, i, 0)),
        out_shape=jax.ShapeDtypeStruct((B, L, C), BF16),
        scratch_shapes=[pltpu.VMEM((tl + CONV_HALO + 8, C), F32), pltpu.VMEM((tl, C), F32)],
        compiler_params=_params(("parallel", "parallel")),
        name="conv_branch",
    )(u3, u3, conv_w, conv_b, ln_g, ln_b, w_out, gates3)


IDX_TQ = 128
IDX_TK = 512


def _sortable(score):
    bits = pltpu.bitcast(score, I32)
    key = bits ^ ((bits >> 31) & 0x7FFFFFFF)
    return jnp.where(score == 0.0, 0, key)


def _index_kernel(qi_ref, ki_ref, kw_ref, mask_ref, key_ref, wb_ref, *, k_top):
    tq = qi_ref.shape[1]
    L = ki_ref.shape[1]
    nkc = L // IDX_TK
    ngrp = IDX_TK // LANES
    i = pl.program_id(1)
    q0 = i * tq
    nk = (q0 + tq + IDX_TK - 1) // IDX_TK

    w = kw_ref[0][:, IDX_DIM:IDX_DIM + IDX_HEADS]
    for h in range(IDX_HEADS):
        wb_ref[h] = jnp.broadcast_to(w[:, h:h + 1], (tq, LANES))

    qpos = q0 + lax.broadcasted_iota(I32, (tq, IDX_TK), 0)
    lane_iota = lax.broadcasted_iota(I32, (tq, IDX_TK), 1)

    def score_body(c, carry):
        k0 = pl.multiple_of(c * IDX_TK, IDX_TK)
        kc = ki_ref[0, pl.ds(k0, IDX_TK), :]
        score = jnp.zeros((tq, IDX_TK), F32)
        for h in range(IDX_HEADS):
            qh = qi_ref[0, :, h * IDX_DIM:(h + 1) * IDX_DIM]
            logit = lax.dot_general(qh, kc, _NT, preferred_element_type=F32)
            wbh = wb_ref[h]
            score = score + jnp.concatenate([wbh] * ngrp, axis=1) * jnp.maximum(logit, 0.0)
        key = _sortable(score)
        key = jnp.where(k0 + lane_iota <= qpos, key, KEY_NEG_INF)
        key_ref[c] = key
        return carry

    lax.fori_loop(0, nk, score_body, 0)

    def count(pred):
        def body(c, acc):
            m = pred(key_ref[c], c)
            for g in range(ngrp):
                acc = acc + jnp.where(m[:, g * LANES:(g + 1) * LANES], 1.0, 0.0)
            return acc
        acc = lax.fori_loop(0, nk, body, jnp.zeros((tq, LANES), F32))
        return jnp.sum(acc, axis=1, keepdims=True)

    def wide(col):
        return jnp.broadcast_to(col, (tq, IDX_TK))

    kf = float(k_top)

    def bit_body(it, t_u):
        bit = lax.shift_left(jnp.int32(1), 31 - it)
        cand_u = t_u | bit
        cand_s = wide(cand_u ^ INT_MIN)
        cnt = count(lambda kc, c: kc >= cand_s)
        return jnp.where(cnt >= kf, cand_u, t_u)

    t_u = lax.fori_loop(0, 32, bit_body, jnp.zeros((tq, 1), I32))
    t_s = t_u ^ INT_MIN
    t_w = wide(t_s)

    cnt_gt = count(lambda kc, c: kc > t_w)
    cnt_ge = count(lambda kc, c: kc >= t_w)
    need = kf - cnt_gt
    tied = cnt_ge > kf

    nbits = max(1, (L - 1).bit_length())

    def cut_body(it, cut):
        bit = lax.shift_left(jnp.int32(1), nbits - 1 - it)
        cand = cut | bit
        cand_w = wide(cand)
        cnt = count(lambda kc, c: (kc == t_w) & (c * IDX_TK + lane_iota < cand_w))
        return jnp.where(cnt < need, cand, cut)

    any_tied = jnp.max(jnp.where(tied, 1.0, 0.0)) > 0.0
    cut = lax.cond(any_tied,
                   lambda: lax.fori_loop(0, nbits, cut_body, jnp.zeros((tq, 1), I32)),
                   lambda: jnp.zeros((tq, 1), I32))
    cut_w = wide(jnp.where(tied, cut, L))

    def write_body(c, carry):
        k0 = pl.multiple_of(c * IDX_TK, IDX_TK)
        kc = key_ref[c]
        kpos = k0 + lane_iota
        sel = (kc > t_w) | ((kc == t_w) & (kpos <= cut_w))
        sel = sel & (kpos <= qpos)
        mask_ref[0, :, pl.ds(k0, IDX_TK)] = jnp.where(sel, 1, 0).astype(jnp.int8)
        return carry

    lax.fori_loop(0, nk, write_body, 0)

    def zero_body(c, carry):
        k0 = pl.multiple_of(c * IDX_TK, IDX_TK)
        mask_ref[0, :, pl.ds(k0, IDX_TK)] = jnp.zeros((tq, IDX_TK), jnp.int8)
        return carry

    lax.fori_loop(nk, nkc, zero_body, 0)


def _index_mask(qi3, ki3, kw3, k_top):
    B, L, nqi = qi3.shape
    tq = IDX_TQ
    return pl.pallas_call(
        functools.partial(_index_kernel, k_top=k_top),
        grid=(B, L // tq),
        in_specs=[pl.BlockSpec((1, tq, nqi), lambda b, i: (b, i, 0)),
                  pl.BlockSpec((1, L, IDX_DIM), lambda b, i: (b, 0, 0)),
                  pl.BlockSpec((1, tq, LANES), lambda b, i: (b, i, 0))],
        out_specs=pl.BlockSpec((1, tq, L), lambda b, i: (b, i, 0)),
        out_shape=jax.ShapeDtypeStruct((B, L, L), jnp.int8),
        scratch_shapes=[pltpu.VMEM((L // IDX_TK, tq, IDX_TK), I32),
                        pltpu.VMEM((IDX_HEADS, tq, LANES), F32)],
        compiler_params=_params(("parallel", "parallel")),
        name="index_mask",
    )(qi3, ki3, kw3)


ATT_TQ = 256
ATT_TK = 512


def _attn_kernel(q_ref, k_ref, v_ref, m_ref, o_ref, acc_ref, mx_ref, l_ref):
    tq = q_ref.shape[1]
    tk = k_ref.shape[1]
    ngrp = tk // LANES
    i = pl.program_id(1)
    kb = pl.program_id(2)
    last = ((i + 1) * tq - 1) // tk

    @pl.when(kb == 0)
    def _():
        acc_ref[...] = jnp.zeros_like(acc_ref)
        mx_ref[...] = jnp.full_like(mx_ref, -jnp.inf)
        l_ref[...] = jnp.zeros_like(l_ref)

    @pl.when(kb <= last)
    def _():
        bias = jnp.where(m_ref[0].astype(I32) != 0, 0.0, -jnp.inf).astype(F32)
        for h in range(N_HEADS):
            hs = slice(h * HEAD_DIM, (h + 1) * HEAD_DIM)
            s = lax.dot_general(q_ref[0, :, hs], k_ref[0, :, hs], _NT,
                                preferred_element_type=F32) + bias
            m_old = mx_ref[h]
            m_new = jnp.maximum(m_old, jnp.max(s, axis=1, keepdims=True))
            m_safe = jnp.where(m_new == -jnp.inf, 0.0, m_new)
            alpha = jnp.exp(m_old - m_safe)
            p = jnp.exp(s - jnp.concatenate([m_safe] * ngrp, axis=1))
            l_ref[h] = alpha * l_ref[h] + jnp.sum(p, axis=1, keepdims=True)
            acc_ref[:, hs] = alpha * acc_ref[:, hs] + jnp.dot(
                p.astype(BF16), v_ref[0, :, hs], preferred_element_type=F32)
            mx_ref[h] = m_new

    @pl.when(kb == last)
    def _():
        for h in range(N_HEADS):
            hs = slice(h * HEAD_DIM, (h + 1) * HEAD_DIM)
            o_ref[0, :, hs] = (acc_ref[:, hs] / l_ref[h]).astype(BF16)


def _sparse_attention(qkv3, mask):
    B, L, _ = qkv3.shape
    W = N_HEADS * HEAD_DIM
    tq, tk = ATT_TQ, ATT_TK

    def kclamp(i, kb):
        return jnp.minimum(kb, ((i + 1) * tq - 1) // tk)

    return pl.pallas_call(
        _attn_kernel,
        grid=(B, L // tq, L // tk),
        in_specs=[pl.BlockSpec((1, tq, W), lambda b, i, kb: (b, i, 0)),
                  pl.BlockSpec((1, tk, W), lambda b, i, kb: (b, kclamp(i, kb), 1)),
                  pl.BlockSpec((1, tk, W), lambda b, i, kb: (b, kclamp(i, kb), 2)),
                  pl.BlockSpec((1, tq, tk), lambda b, i, kb: (b, i, kclamp(i, kb)))],
        out_specs=pl.BlockSpec((1, tq, W), lambda b, i, kb: (b, i, 0)),
        out_shape=jax.ShapeDtypeStruct((B, L, W), BF16),
        scratch_shapes=[pltpu.VMEM((tq, W), F32),
                        pltpu.VMEM((N_HEADS, tq, LANES), F32),
                        pltpu.VMEM((N_HEADS, tq, LANES), F32)],
        compiler_params=_params(("parallel", "parallel", "arbitrary")),
        name="sparse_attn",
    )(qkv3, qkv3, qkv3, mask)


def _mix_kernel(x_ref, attn_ref, yc_ref, ga_ref, wao_ref, wmo_ref, o_ref):
    ya = jnp.dot(attn_ref[...], wao_ref[...], preferred_element_type=F32)
    m = yc_ref[...].astype(F32) + ga_ref[...].astype(F32) * ya
    o_ref[...] = x_ref[...] + jnp.dot(m.astype(BF16), wmo_ref[...], preferred_element_type=F32)


def _mix(x2, attn2, yc2, gates2, w_ao, w_mo, tm):
    T, D = x2.shape
    row = lambda i: (i, 0)
    return pl.pallas_call(
        _mix_kernel,
        grid=(T // tm,),
        in_specs=[pl.BlockSpec((tm, D), row),
                  pl.BlockSpec((tm, D), row),
                  pl.BlockSpec((tm, D), row),
                  pl.BlockSpec((tm, D), lambda i: (i, 1)),
                  pl.BlockSpec((D, D), lambda i: (0, 0)),
                  pl.BlockSpec((D, D), lambda i: (0, 0))],
        out_specs=pl.BlockSpec((tm, D), row),
        out_shape=jax.ShapeDtypeStruct((T, D), F32),
        compiler_params=_params(("parallel",)),
        name="mix_out",
    )(x2, attn2, yc2, gates2, w_ao, w_mo)


def _memkv_kernel(mem_ref, g_ref, w_ref, o_ref):
    mn = _rms_bf16(mem_ref[0], g_ref[...])
    o_ref[0] = jnp.dot(mn, w_ref[...], preferred_element_type=F32).astype(BF16)


def _xattn_kernel(h_ref, g_ref, wq_ref, kv_ref, wo_ref, o_ref):
    h = h_ref[0]
    hn = _rms_bf16(h, g_ref[...])
    q = (jnp.dot(hn, wq_ref[...], preferred_element_type=F32) * (X_HEAD_DIM ** -0.5)).astype(BF16)
    outs = []
    for hh in range(X_HEADS):
        hs = slice(hh * X_HEAD_DIM, (hh + 1) * X_HEAD_DIM)
        vs = slice(X_W + hh * X_HEAD_DIM, X_W + (hh + 1) * X_HEAD_DIM)
        s = lax.dot_general(q[:, hs], kv_ref[0, :, hs], _NT, preferred_element_type=F32)
        p = jnp.exp(s - jnp.max(s, axis=1, keepdims=True))
        l = jnp.sum(p, axis=1, keepdims=True)
        o = jnp.dot(p.astype(BF16), kv_ref[0, :, vs], preferred_element_type=F32) / l
        outs.append(o.astype(BF16))
    o = jnp.concatenate(outs, axis=1)
    o_ref[0] = h + jnp.dot(o, wo_ref[...], preferred_element_type=F32)


def _cross_attention(h3, mem, g_x, g_mem, wq, wkv, wo, tm):
    B, L, D = h3.shape
    n_mem = mem.shape[1]
    kv = pl.pallas_call(
        _memkv_kernel,
        grid=(B,),
        in_specs=[pl.BlockSpec((1, n_mem, D), lambda b: (b, 0, 0)),
                  pl.BlockSpec((1, D), lambda b: (0, 0)),
                  pl.BlockSpec((D, 2 * X_W), lambda b: (0, 0))],
        out_specs=pl.BlockSpec((1, n_mem, 2 * X_W), lambda b: (b, 0, 0)),
        out_shape=jax.ShapeDtypeStruct((B, n_mem, 2 * X_W), BF16),
        compiler_params=_params(("parallel",)),
        name="mem_kv",
    )(mem, g_mem, wkv)

    return pl.pallas_call(
        _xattn_kernel,
        grid=(B, L // tm),
        in_specs=[pl.BlockSpec((1, tm, D), lambda b, i: (b, i, 0)),
                  pl.BlockSpec((1, D), lambda b, i: (0, 0)),
                  pl.BlockSpec((D, X_W), lambda b, i: (0, 0)),
                  pl.BlockSpec((1, n_mem, 2 * X_W), lambda b, i: (b, 0, 0)),
                  pl.BlockSpec((X_W, D), lambda b, i: (0, 0))],
        out_specs=pl.BlockSpec((1, tm, D), lambda b, i: (b, i, 0)),
        out_shape=jax.ShapeDtypeStruct((B, L, D), F32),
        compiler_params=_params(("parallel", "parallel")),
        name="cross_attn",
    )(h3, g_x, wq, kv, wo)


def _ffn_kernel(h_ref, g_ref, w1_ref, w2_ref, gf_ref, o_ref, hn_ref, acc_ref):
    j = pl.program_id(1)

    @pl.when(j == 0)
    def _():
        h = h_ref[...]
        hn_ref[...] = _rms_bf16(h, g_ref[...])
        acc_ref[...] = h

    a = jnp.dot(hn_ref[...], w1_ref[...], preferred_element_type=F32)
    r = jnp.maximum(a, 0.0)
    acc_ref[...] += jnp.dot((r * r).astype(BF16), w2_ref[...], preferred_element_type=F32)

    @pl.when(j == pl.num_programs(1) - 1)
    def _():
        y = acc_ref[...]
        ms = jnp.mean(y * y, axis=-1, keepdims=True)
        o_ref[...] = y * lax.rsqrt(ms + EPS) * gf_ref[...]


def _ffn(h2, g, w1, w2, g_final, tm, tf):
    T, D = h2.shape
    F = w1.shape[1]
    return pl.pallas_call(
        _ffn_kernel,
        grid=(T // tm, F // tf),
        in_specs=[pl.BlockSpec((tm, D), lambda i, j: (i, 0)),
                  pl.BlockSpec((1, D), lambda i, j: (0, 0)),
                  pl.BlockSpec((D, tf), lambda i, j: (0, j)),
                  pl.BlockSpec((tf, D), lambda i, j: (j, 0)),
                  pl.BlockSpec((1, D), lambda i, j: (0, 0))],
        out_specs=pl.BlockSpec((tm, D), lambda i, j: (i, 0)),
        out_shape=jax.ShapeDtypeStruct((T, D), F32),
        scratch_shapes=[pltpu.VMEM((tm, D), BF16), pltpu.VMEM((tm, D), F32)],
        compiler_params=_params(("parallel", "arbitrary")),
        name="ffn",
    )(h2, g, w1, w2, g_final)


def _layer(h3, mem, norm_mix_g, w_in, b_gate, conv_w, conv_b, conv_ln_g, conv_ln_b,
           w_conv_out, w_attn_out, w_mix_out, norm_x_g, norm_mem_g, wx_q, wx_kv, wx_o,
           norm_ffn_g, w_ff1, w_ff2, final_g):
    B, L, D = h3.shape
    T = B * L
    tm = 512
    k_top = min(TOPK_MAX, L // 4)
    row = lambda v: v.reshape(1, -1)

    o = 0
    w_a = w_in[:, o:o + D].astype(BF16); o += D
    w_g = w_in[:, o:o + D].astype(BF16); o += D
    w_qkv = w_in[:, o:o + 3 * D].astype(BF16); o += 3 * D
    w_qi = w_in[:, o:o + IDX_HEADS * IDX_DIM].astype(BF16); o += IDX_HEADS * IDX_DIM
    n_kw = IDX_DIM + IDX_HEADS
    w_kw = jnp.pad(w_in[:, o:o + n_kw], ((0, 0), (0, LANES - n_kw))).astype(BF16); o += n_kw
    w_gate = w_in[:, o:o + 2 * D].astype(BF16)

    x2 = h3.reshape(T, D)
    u, qkv, qi, ki, kw, gates = _projections(
        x2, row(norm_mix_g), w_a, w_g, w_qkv, w_qi, w_kw, w_gate, row(b_gate), tm)

    yc = _conv_branch(u.reshape(B, L, D), gates.reshape(B, L, 2 * D), conv_w, row(conv_b),
                      row(conv_ln_g), row(conv_ln_b), w_conv_out.astype(BF16), 512)

    mask = _index_mask(qi.reshape(B, L, -1), ki.reshape(B, L, IDX_DIM),
                       kw.reshape(B, L, LANES), k_top)
    attn = _sparse_attention(qkv.reshape(B, L, 3 * D), mask)

    h1 = _mix(x2, attn.reshape(T, D), yc.reshape(T, D), gates,
              w_attn_out.astype(BF16), w_mix_out.astype(BF16), tm)

    h2 = _cross_attention(h1.reshape(B, L, D), mem, row(norm_x_g), row(norm_mem_g),
                          wx_q.astype(BF16), wx_kv.astype(BF16), wx_o.astype(BF16), tm)

    out = _ffn(h2.reshape(T, D), row(norm_ffn_g), w_ff1.astype(BF16), w_ff2.astype(BF16),
               final_g, tm, 1024)
    return out.reshape(B, L, D)


def kernel(x, mem, norm_mix_g, w_in, b_gate, conv_w, conv_b, conv_ln_g, conv_ln_b, w_conv_out,
           w_attn_out, w_mix_out, norm_x_g, norm_mem_g, wx_q, wx_kv, wx_o, norm_ffn_g, w_ff1,
           w_ff2, norm_final_g):
    depth = w_in.shape[0]
    assert depth == 1, "final RMSNorm is fused into the last layer's MLP kernel"
    assert x.shape[-1] == D_MODEL
    return _layer(x, mem, norm_mix_g[0], w_in[0], b_gate[0], conv_w[0], conv_b[0], conv_ln_g[0],
                  conv_ln_b[0], w_conv_out[0], w_attn_out[0], w_mix_out[0], norm_x_g[0],
                  norm_mem_g[0], wx_q[0], wx_kv[0], wx_o[0], norm_ffn_g[0], w_ff1[0], w_ff2[0],
                  norm_final_g.reshape(1, -1))
```

```python
import functools
import math

import numpy as np
import jax
import jax.numpy as jnp
from jax import lax
from jax.experimental import pallas as pl
from jax.experimental.pallas import tpu as pltpu

F32 = jnp.float32
BF16 = jnp.bfloat16
I32 = jnp.int32

EPS = 1e-6
D_MODEL = 1024
N_HEADS = 8
HEAD_DIM = 128
IDX_HEADS = 8
IDX_DIM = 64
TOPK_MAX = 256
CONV_WIDTH = 31
X_HEADS = 4
X_HEAD_DIM = 128
X_W = X_HEADS * X_HEAD_DIM
LANES = 128
LOG2E = math.log2(math.e)

INT_MIN = -(2 ** 31)
KEY_NEG_INF = int(np.int32(np.uint32(0xFF800000) ^ np.uint32(0x7FFFFFFF)))
HI_NEG_INF = KEY_NEG_INF >> 16
HI_MIN_NORMAL = 0x0080

VMEM_LIMIT = 56 * 1024 * 1024

_NT = (((1,), (1,)), ((), ()))


def _params(sem):
    return pltpu.CompilerParams(dimension_semantics=sem, vmem_limit_bytes=VMEM_LIMIT)


def _rms_bf16(x, g):
    ms = jnp.mean(x * x, axis=-1, keepdims=True)
    return (x * lax.rsqrt(ms + EPS) * g).astype(BF16)


def _sigmoid(x):
    return 1.0 / (1.0 + jnp.exp(-x))


N_QI = IDX_HEADS * IDX_DIM
OFF_A = 0
OFF_G = OFF_A + D_MODEL
OFF_QKV = OFF_G + D_MODEL
OFF_QI = OFF_QKV + 3 * D_MODEL
OFF_KW = OFF_QI + N_QI
OFF_GATE = OFF_KW + LANES
PROJ_WIDTH = OFF_GATE + 2 * D_MODEL
PROJ_TN = 512


def _proj_kernel(x_ref, g_ref, w_ref, b_ref, u_ref, qkv_ref, qi_ref, ki_ref, kw_ref, gate_ref):
    D = D_MODEL
    xn = _rms_bf16(x_ref[...], g_ref[...])

    def mm(lo, width):
        return jnp.dot(xn, w_ref[:, lo:lo + width], preferred_element_type=F32)

    for c in range(0, D, PROJ_TN):
        u_ref[:, c:c + PROJ_TN] = (mm(OFF_A + c, PROJ_TN) *
                                   _sigmoid(mm(OFF_G + c, PROJ_TN))).astype(BF16)
    for c in range(0, 3 * D, PROJ_TN):
        scale = HEAD_DIM ** -0.5 * LOG2E if c < D else 1.0
        qkv_ref[:, c:c + PROJ_TN] = (mm(OFF_QKV + c, PROJ_TN) * scale).astype(BF16)
    qi_ref[...] = (mm(OFF_QI, N_QI) * (IDX_DIM ** -0.5)).astype(BF16)
    kw = mm(OFF_KW, LANES)
    ki_ref[...] = kw[:, :IDX_DIM].astype(BF16)
    kw_ref[...] = kw * (IDX_HEADS ** -0.5)
    for c in range(0, 2 * D, PROJ_TN):
        gate_ref[:, c:c + PROJ_TN] = _sigmoid(
            mm(OFF_GATE + c, PROJ_TN) + b_ref[:, c:c + PROJ_TN]).astype(BF16)


def _projections(x2, g, w_all, b_gate, tm):
    T, D = x2.shape
    row = lambda i: (i, 0)
    fixed = lambda i: (0, 0)
    widths = (D, 3 * D, N_QI, IDX_DIM, LANES, 2 * D)
    dtypes = (BF16, BF16, BF16, BF16, F32, BF16)
    return pl.pallas_call(
        _proj_kernel,
        grid=(T // tm,),
        in_specs=[pl.BlockSpec((tm, D), row),
                  pl.BlockSpec((1, D), fixed),
                  pl.BlockSpec((D, PROJ_WIDTH), fixed),
                  pl.BlockSpec((1, 2 * D), fixed)],
        out_specs=[pl.BlockSpec((tm, w), row) for w in widths],
        out_shape=[jax.ShapeDtypeStruct((T, w), dt) for w, dt in zip(widths, dtypes)],
        compiler_params=_params(("parallel",)),
        name="proj_in",
    )(x2, g, w_all, b_gate)


CONV_HALO = 32
CONV_ROWS = 32
CONV_LANES = 256


def _conv_kernel(u_ref, halo_ref, cw_ref, cb_ref, lng_ref, lnb_ref, wout_ref, gc_ref,
                 o_ref, win_ref, cv_ref):
    tl = u_ref.shape[1]
    C = u_ref.shape[2]
    i = pl.program_id(1)
    halo = halo_ref[0].astype(F32)
    win_ref[0:CONV_HALO, :] = jnp.where(i > 0, halo, 0.0)
    win_ref[CONV_HALO:CONV_HALO + tl, :] = u_ref[0].astype(F32)
    win_ref[CONV_HALO + tl:, :] = jnp.zeros((8, C), F32)

    s0 = CONV_HALO - (CONV_WIDTH - 1)
    for lc in range(C // CONV_LANES):
        lanes = slice(lc * CONV_LANES, (lc + 1) * CONV_LANES)

        def body(r, carry, lanes=lanes):
            r0 = pl.multiple_of(r * CONV_ROWS, CONV_ROWS)
            out = jnp.zeros((CONV_ROWS, CONV_LANES), F32)
            for res in range(8):
                part = None
                for j in range(CONV_WIDTH):
                    s = j + s0
                    if s % 8 != res:
                        continue
                    a = s - res
                    term = cw_ref[j:j + 1, lanes] * win_ref[pl.ds(r0 + a, CONV_ROWS + 8), lanes]
                    part = term if part is None else part + term
                if part is not None:
                    out = out + part[res:res + CONV_ROWS, :]
            cv_ref[pl.ds(r0, CONV_ROWS), lanes] = out + cb_ref[:, lanes]
            return carry

        lax.fori_loop(0, tl // CONV_ROWS, body, 0)

    y = cv_ref[...]
    mu = jnp.mean(y, axis=-1, keepdims=True)
    yc = y - mu
    var = jnp.mean(yc * yc, axis=-1, keepdims=True)
    z = yc * lax.rsqrt(var + EPS) * lng_ref[...] + lnb_ref[...]
    act = (z * _sigmoid(z)).astype(BF16)
    out = jnp.dot(act, wout_ref[...], preferred_element_type=F32)
    o_ref[0] = (gc_ref[0].astype(F32) * out).astype(BF16)


def _conv_branch(u3, gates3, conv_w, conv_b, ln_g, ln_b, w_out, tl):
    B, L, C = u3.shape
    hb = tl // CONV_HALO
    return pl.pallas_call(
        _conv_kernel,
        grid=(B, L // tl),
        in_specs=[pl.BlockSpec((1, tl, C), lambda b, i: (b, i, 0)),
                  pl.BlockSpec((1, CONV_HALO, C), lambda b, i: (b, jnp.maximum(i * hb - 1, 0), 0)),
                  pl.BlockSpec((CONV_WIDTH, C), lambda b, i: (0, 0)),
                  pl.BlockSpec((1, C), lambda b, i: (0, 0)),
                  pl.BlockSpec((1, C), lambda b, i: (0, 0)),
                  pl.BlockSpec((1, C), lambda b, i: (0, 0)),
                  pl.BlockSpec((C, C), lambda b, i: (0, 0)),
                  pl.BlockSpec((1, tl, C), lambda b, i: (b, i, 0))],
        out_specs=pl.BlockSpec((1, tl, C), lambda b, i: (b, i, 0)),
        out_shape=jax.ShapeDtypeStruct((B, L, C), BF16),
        scratch_shapes=[pltpu.VMEM((tl + CONV_HALO + 8, C), F32), pltpu.VMEM((tl, C), F32)],
        compiler_params=_params(("parallel", "parallel")),
        name="conv_branch",
    )(u3, u3, conv_w, conv_b, ln_g, ln_b, w_out, gates3)


IDX_TQ = 256
IDX_TK = 512


HI_BITS = -65536
IDX_ROWS = 128


def _hi16_as_float(hi):
    pat = hi ^ ((hi >> 31) & 0x7FFF)
    return pltpu.bitcast(pat << 16, F32)


def _index_kernel(qi_ref, ki_ref, kw_ref, mask_ref, key_ref, hi_ref, wb_ref, *, k_top):
    tq = qi_ref.shape[1]
    L = ki_ref.shape[1]
    nkc = L // IDX_TK
    ngrp = IDX_TK // LANES
    i = pl.program_id(1)
    q0 = i * tq
    nk = (q0 + tq + IDX_TK - 1) // IDX_TK
    kf = float(k_top)

    w = kw_ref[0][:, IDX_DIM:IDX_DIM + IDX_HEADS]
    for h in range(IDX_HEADS):
        wb_ref[h] = jnp.broadcast_to(w[:, h:h + 1], (tq, LANES))

    qpos = q0 + lax.broadcasted_iota(I32, (tq, IDX_TK), 0)
    lane_iota = lax.broadcasted_iota(I32, (tq, IDX_TK), 1)
    group_iota = lax.broadcasted_iota(I32, (tq, LANES), 1)

    def score_chunk(c, diagonal):
        k0 = pl.multiple_of(c * IDX_TK, IDX_TK)
        kc = ki_ref[0, pl.ds(k0, IDX_TK), :]
        score = jnp.zeros((tq, IDX_TK), F32)
        for h in range(IDX_HEADS):
            qh = qi_ref[0, :, h * IDX_DIM:(h + 1) * IDX_DIM]
            logit = lax.dot_general(qh, kc, _NT, preferred_element_type=F32)
            wbh = wb_ref[h]
            score = score + jnp.concatenate([wbh] * ngrp, axis=1) * jnp.maximum(logit, 0.0)
        bits = pltpu.bitcast(score, I32)
        bits = jnp.where((bits & 0x7F800000) == 0, 0, bits)
        key = bits ^ ((bits >> 31) & 0x7FFFFFFF)
        hi_bits = bits & HI_BITS
        if diagonal:
            causal = k0 + lane_iota <= qpos
            key = jnp.where(causal, key, INT_MIN)
            hi_bits = jnp.where(causal, hi_bits, HI_BITS)
        key_ref[c] = key
        hi_ref[c] = pltpu.bitcast(hi_bits, F32).astype(BF16)

    def score_body(c, carry):
        score_chunk(c, False)
        return carry

    lax.fori_loop(0, nk - 1, score_body, 0)
    score_chunk(nk - 1, True)

    ones_mat = jnp.ones((LANES, LANES), BF16)

    def rowsum(acc):
        return jnp.dot(acc.astype(BF16), ones_mat, preferred_element_type=F32)

    def count_hi(cand):
        one = jnp.ones((tq, LANES), BF16)
        zero = jnp.zeros((tq, LANES), BF16)

        def body(c, acc):
            hc = hi_ref[c]
            for g in range(ngrp):
                acc = acc + jnp.where(hc[:, g * LANES:(g + 1) * LANES] >= cand, one, zero)
            return acc
        return rowsum(lax.fori_loop(0, nk, body, zero))

    def count_key(pred, *row_args):
        parts = []
        for r0 in range(0, tq, IDX_ROWS):
            rows = slice(r0, r0 + IDX_ROWS)
            args = [a[rows] for a in row_args]
            pos = lax.broadcasted_iota(I32, (IDX_ROWS, LANES), 1)

            def body(c, acc, rows=rows, args=args, pos=pos):
                for g in range(ngrp):
                    kg = key_ref[c, rows, g * LANES:(g + 1) * LANES]
                    m = pred(kg, c * IDX_TK + g * LANES + pos, *args)
                    acc = acc + jnp.where(m, 1.0, 0.0)
                return acc
            parts.append(lax.fori_loop(0, nk, body, jnp.zeros((IDX_ROWS, LANES), F32)))
        return rowsum(jnp.concatenate(parts, axis=0))

    def hi_body(it, carry):
        t_hi, cnt_t = carry
        cand_u = t_hi | lax.shift_left(jnp.int32(1), 15 - it)
        cand_s = cand_u - 32768
        cand_s = jnp.where((cand_s >= 1) & (cand_s < HI_MIN_NORMAL), HI_MIN_NORMAL, cand_s)
        cand_s = jnp.where((cand_s >= -HI_MIN_NORMAL) & (cand_s <= -1), 0, cand_s)
        cand = jnp.where(cand_s < HI_NEG_INF, -jnp.inf, _hi16_as_float(cand_s)).astype(BF16)
        cnt = count_hi(cand)
        take = cnt >= kf
        return jnp.where(take, cand_u, t_hi), jnp.where(take, cnt, cnt_t)

    t_hi, cnt_t = lax.fori_loop(
        0, 16, hi_body, (jnp.zeros((tq, LANES), I32), jnp.zeros((tq, LANES), F32)))

    def unsettled(cnt):
        return jnp.max(jnp.where(cnt > kf, 1.0, 0.0)) > 0.0

    def lo_cond(carry):
        it, _, _, go = carry
        return jnp.logical_and(it < 16, go)

    def lo_body(carry):
        it, t_u, cnt_t, _ = carry
        cand_u = t_u | lax.shift_left(jnp.int32(1), 15 - it)
        cand_s = cand_u ^ INT_MIN
        cnt = count_key(lambda kg, kpos, cs: kg >= cs, cand_s)
        take = cnt >= kf
        cnt_t = jnp.where(take, cnt, cnt_t)
        return it + 1, jnp.where(take, cand_u, t_u), cnt_t, unsettled(cnt_t)

    _, t_u, cnt_t, _ = lax.while_loop(
        lo_cond, lo_body, (jnp.int32(0), t_hi << 16, cnt_t, unsettled(cnt_t)))

    t_s = jnp.maximum(t_u ^ INT_MIN, INT_MIN + 1)
    tied = cnt_t > kf
    neg_inf = jnp.float32(-jnp.inf)

    def write_plain():
        def body(c, carry):
            k0 = pl.multiple_of(c * IDX_TK, IDX_TK)
            kc = key_ref[c]
            cols = [jnp.where(kc[:, g * LANES:(g + 1) * LANES] >= t_s, 0.0, neg_inf)
                    for g in range(ngrp)]
            mask_ref[0, :, pl.ds(k0, IDX_TK)] = jnp.concatenate(cols, axis=1).astype(BF16)
            return carry
        lax.fori_loop(0, nk, body, 0)

    def write_tied():
        need = kf - count_key(lambda kg, kpos, ts: kg > ts, t_s)
        nbits = max(1, (L - 1).bit_length())

        def cut_body(it, cut):
            cand = cut | lax.shift_left(jnp.int32(1), nbits - 1 - it)
            cnt = count_key(lambda kg, kpos, ts, cd: (kg == ts) & (kpos < cd), t_s, cand)
            return jnp.where(cnt < need, cand, cut)

        cut = lax.fori_loop(0, nbits, cut_body, jnp.zeros((tq, LANES), I32))
        cut = jnp.where(tied, cut, L)

        def body(c, carry):
            k0 = pl.multiple_of(c * IDX_TK, IDX_TK)
            kc = key_ref[c]
            cols = []
            for g in range(ngrp):
                sl = slice(g * LANES, (g + 1) * LANES)
                kg = kc[:, sl]
                sel = (kg > t_s) | ((kg == t_s) & (k0 + g * LANES + group_iota <= cut))
                cols.append(jnp.where(sel, 0.0, neg_inf))
            mask_ref[0, :, pl.ds(k0, IDX_TK)] = jnp.concatenate(cols, axis=1).astype(BF16)
            return carry
        lax.fori_loop(0, nk, body, 0)

    any_tied = jnp.max(jnp.where(tied, 1.0, 0.0)) > 0.0
    lax.cond(any_tied, write_tied, write_plain)

    def fill_body(c, carry):
        k0 = pl.multiple_of(c * IDX_TK, IDX_TK)
        mask_ref[0, :, pl.ds(k0, IDX_TK)] = jnp.full((tq, IDX_TK), -jnp.inf, BF16)
        return carry

    lax.fori_loop(nk, nkc, fill_body, 0)


def _index_mask(qi3, ki3, kw3, k_top):
    B, L, nqi = qi3.shape
    tq = IDX_TQ
    assert IDX_TK % tq == 0 and L % IDX_TK == 0 and tq % IDX_ROWS == 0
    assert L // LANES <= 256, "per-lane counts must stay exact in bf16"
    return pl.pallas_call(
        functools.partial(_index_kernel, k_top=k_top),
        grid=(B, L // tq),
        in_specs=[pl.BlockSpec((1, tq, nqi), lambda b, i: (b, i, 0)),
                  pl.BlockSpec((1, L, IDX_DIM), lambda b, i: (b, 0, 0)),
                  pl.BlockSpec((1, tq, LANES), lambda b, i: (b, i, 0))],
        out_specs=pl.BlockSpec((1, tq, L), lambda b, i: (b, i, 0)),
        out_shape=jax.ShapeDtypeStruct((B, L, L), BF16),
        scratch_shapes=[pltpu.VMEM((L // IDX_TK, tq, IDX_TK), I32),
                        pltpu.VMEM((L // IDX_TK, tq, IDX_TK), BF16),
                        pltpu.VMEM((IDX_HEADS, tq, LANES), F32)],
        compiler_params=_params(("parallel", "parallel")),
        name="index_mask",
    )(qi3, ki3, kw3)


ATT_TQ = 512
ATT_TK = 1024


def _attn_kernel(q_ref, k_ref, v_ref, m_ref, o_ref, acc_ref, mx_ref):
    tq = q_ref.shape[1]
    tk = k_ref.shape[1]
    ngrp = tk // LANES
    i = pl.program_id(1)
    kb = pl.program_id(2)
    last = ((i + 1) * tq - 1) // tk

    @pl.when(kb == 0)
    def _():
        acc_ref[...] = jnp.zeros_like(acc_ref)
        mx_ref[...] = jnp.full_like(mx_ref, -jnp.inf)

    @pl.when(kb <= last)
    def _():
        bias = m_ref[0].astype(F32)
        ones = jnp.ones((tk, HEAD_DIM), BF16)
        for h in range(N_HEADS):
            hs = slice(h * HEAD_DIM, (h + 1) * HEAD_DIM)
            s = lax.dot_general(q_ref[0, :, hs], k_ref[0, :, hs], _NT,
                                preferred_element_type=F32) + bias
            m_old = mx_ref[h]
            m_new = jnp.maximum(m_old, jnp.max(s, axis=1, keepdims=True))
            m_safe = jnp.where(m_new == -jnp.inf, 0.0, m_new)
            alpha = jnp.exp2(m_old - m_safe)
            p = jnp.exp2(s - jnp.concatenate([m_safe] * ngrp, axis=1)).astype(BF16)
            v_ext = jnp.concatenate([v_ref[0, :, hs], ones], axis=1)
            acc_ref[h] = jnp.concatenate([alpha, alpha], axis=1) * acc_ref[h] + jnp.dot(
                p, v_ext, preferred_element_type=F32)
            mx_ref[h] = m_new

    @pl.when(kb == last)
    def _():
        for h in range(N_HEADS):
            hs = slice(h * HEAD_DIM, (h + 1) * HEAD_DIM)
            a = acc_ref[h]
            o_ref[0, :, hs] = (a[:, :HEAD_DIM] / a[:, HEAD_DIM:]).astype(BF16)


def _sparse_attention(qkv3, mask):
    B, L, _ = qkv3.shape
    W = N_HEADS * HEAD_DIM
    tq, tk = ATT_TQ, ATT_TK

    def kclamp(i, kb):
        return jnp.minimum(kb, ((i + 1) * tq - 1) // tk)

    return pl.pallas_call(
        _attn_kernel,
        grid=(B, L // tq, L // tk),
        in_specs=[pl.BlockSpec((1, tq, W), lambda b, i, kb: (b, i, 0)),
                  pl.BlockSpec((1, tk, W), lambda b, i, kb: (b, kclamp(i, kb), 1)),
                  pl.BlockSpec((1, tk, W), lambda b, i, kb: (b, kclamp(i, kb), 2)),
                  pl.BlockSpec((1, tq, tk), lambda b, i, kb: (b, i, kclamp(i, kb)))],
        out_specs=pl.BlockSpec((1, tq, W), lambda b, i, kb: (b, i, 0)),
        out_shape=jax.ShapeDtypeStruct((B, L, W), BF16),
        scratch_shapes=[pltpu.VMEM((N_HEADS, tq, 2 * HEAD_DIM), F32),
                        pltpu.VMEM((N_HEADS, tq, LANES), F32)],
        compiler_params=_params(("parallel", "parallel", "arbitrary")),
        name="sparse_attn",
    )(qkv3, qkv3, qkv3, mask)


def _mix_kernel(x_ref, attn_ref, yc_ref, ga_ref, wao_ref, wmo_ref, o_ref):
    ya = jnp.dot(attn_ref[...], wao_ref[...], preferred_element_type=F32)
    m = yc_ref[...].astype(F32) + ga_ref[...].astype(F32) * ya
    o_ref[...] = x_ref[...] + jnp.dot(m.astype(BF16), wmo_ref[...], preferred_element_type=F32)


def _mix(x2, attn2, yc2, gates2, w_ao, w_mo, tm):
    T, D = x2.shape
    row = lambda i: (i, 0)
    return pl.pallas_call(
        _mix_kernel,
        grid=(T // tm,),
        in_specs=[pl.BlockSpec((tm, D), row),
                  pl.BlockSpec((tm, D), row),
                  pl.BlockSpec((tm, D), row),
                  pl.BlockSpec((tm, D), lambda i: (i, 1)),
                  pl.BlockSpec((D, D), lambda i: (0, 0)),
                  pl.BlockSpec((D, D), lambda i: (0, 0))],
        out_specs=pl.BlockSpec((tm, D), row),
        out_shape=jax.ShapeDtypeStruct((T, D), F32),
        compiler_params=_params(("parallel",)),
        name="mix_out",
    )(x2, attn2, yc2, gates2, w_ao, w_mo)


def _memkv_kernel(mem_ref, g_ref, w_ref, o_ref):
    mn = _rms_bf16(mem_ref[0], g_ref[...])
    o_ref[0] = jnp.dot(mn, w_ref[...], preferred_element_type=F32).astype(BF16)


def _xattn_kernel(h_ref, g_ref, wq_ref, kv_ref, wo_ref, o_ref):
    h = h_ref[0]
    hn = _rms_bf16(h, g_ref[...])
    q = (jnp.dot(hn, wq_ref[...], preferred_element_type=F32) * (X_HEAD_DIM ** -0.5)).astype(BF16)
    outs = []
    for hh in range(X_HEADS):
        hs = slice(hh * X_HEAD_DIM, (hh + 1) * X_HEAD_DIM)
        vs = slice(X_W + hh * X_HEAD_DIM, X_W + (hh + 1) * X_HEAD_DIM)
        s = lax.dot_general(q[:, hs], kv_ref[0, :, hs], _NT, preferred_element_type=F32)
        p = jnp.exp(s - jnp.max(s, axis=1, keepdims=True))
        l = jnp.sum(p, axis=1, keepdims=True)
        o = jnp.dot(p.astype(BF16), kv_ref[0, :, vs], preferred_element_type=F32) / l
        outs.append(o.astype(BF16))
    o = jnp.concatenate(outs, axis=1)
    o_ref[0] = h + jnp.dot(o, wo_ref[...], preferred_element_type=F32)


def _cross_attention(h3, mem, g_x, g_mem, wq, wkv, wo, tm):
    B, L, D = h3.shape
    n_mem = mem.shape[1]
    kv = pl.pallas_call(
        _memkv_kernel,
        grid=(B,),
        in_specs=[pl.BlockSpec((1, n_mem, D), lambda b: (b, 0, 0)),
                  pl.BlockSpec((1, D), lambda b: (0, 0)),
                  pl.BlockSpec((D, 2 * X_W), lambda b: (0, 0))],
        out_specs=pl.BlockSpec((1, n_mem, 2 * X_W), lambda b: (b, 0, 0)),
        out_shape=jax.ShapeDtypeStruct((B, n_mem, 2 * X_W), BF16),
        compiler_params=_params(("parallel",)),
        name="mem_kv",
    )(mem, g_mem, wkv)

    return pl.pallas_call(
        _xattn_kernel,
        grid=(B, L // tm),
        in_specs=[pl.BlockSpec((1, tm, D), lambda b, i: (b, i, 0)),
                  pl.BlockSpec((1, D), lambda b, i: (0, 0)),
                  pl.BlockSpec((D, X_W), lambda b, i: (0, 0)),
                  pl.BlockSpec((1, n_mem, 2 * X_W), lambda b, i: (b, 0, 0)),
                  pl.BlockSpec((X_W, D), lambda b, i: (0, 0))],
        out_specs=pl.BlockSpec((1, tm, D), lambda b, i: (b, i, 0)),
        out_shape=jax.ShapeDtypeStruct((B, L, D), F32),
        compiler_params=_params(("parallel", "parallel")),
        name="cross_attn",
    )(h3, g_x, wq, kv, wo)


def _ffn_kernel(h_ref, g_ref, w1_ref, w2_ref, gf_ref, o_ref, hn_ref, acc_ref):
    j = pl.program_id(1)

    @pl.when(j == 0)
    def _():
        h = h_ref[...]
        hn_ref[...] = _rms_bf16(h, g_ref[...])
        acc_ref[...] = h

    a = jnp.dot(hn_ref[...], w1_ref[...], preferred_element_type=F32)
    r = jnp.maximum(a, 0.0)
    acc_ref[...] += jnp.dot((r * r).astype(BF16), w2_ref[...], preferred_element_type=F32)

    @pl.when(j == pl.num_programs(1) - 1)
    def _():
        y = acc_ref[...]
        ms = jnp.mean(y * y, axis=-1, keepdims=True)
        o_ref[...] = y * lax.rsqrt(ms + EPS) * gf_ref[...]


def _ffn(h2, g, w1, w2, g_final, tm, tf):
    T, D = h2.shape
    F = w1.shape[1]
    return pl.pallas_call(
        _ffn_kernel,
        grid=(T // tm, F // tf),
        in_specs=[pl.BlockSpec((tm, D), lambda i, j: (i, 0)),
                  pl.BlockSpec((1, D), lambda i, j: (0, 0)),
                  pl.BlockSpec((D, tf), lambda i, j: (0, j)),
                  pl.BlockSpec((tf, D), lambda i, j: (j, 0)),
                  pl.BlockSpec((1, D), lambda i, j: (0, 0))],
        out_specs=pl.BlockSpec((tm, D), lambda i, j: (i, 0)),
        out_shape=jax.ShapeDtypeStruct((T, D), F32),
        scratch_shapes=[pltpu.VMEM((tm, D), BF16), pltpu.VMEM((tm, D), F32)],
        compiler_params=_params(("parallel", "arbitrary")),
        name="ffn",
    )(h2, g, w1, w2, g_final)


def _layer(h3, mem, norm_mix_g, w_in, b_gate, conv_w, conv_b, conv_ln_g, conv_ln_b,
           w_conv_out, w_attn_out, w_mix_out, norm_x_g, norm_mem_g, wx_q, wx_kv, wx_o,
           norm_ffn_g, w_ff1, w_ff2, final_g):
    B, L, D = h3.shape
    T = B * L
    tm = 512
    k_top = min(TOPK_MAX, L // 4)
    row = lambda v: v.reshape(1, -1)

    n_kw = IDX_DIM + IDX_HEADS
    w_all = jnp.concatenate(
        [w_in[:, :OFF_KW], jnp.pad(w_in[:, OFF_KW:OFF_KW + n_kw], ((0, 0), (0, LANES - n_kw))),
         w_in[:, OFF_KW + n_kw:]], axis=1).astype(BF16)
    assert w_all.shape[1] == PROJ_WIDTH

    x2 = h3.reshape(T, D)
    u, qkv, qi, ki, kw, gates = _projections(x2, row(norm_mix_g), w_all, row(b_gate), tm)

    yc = _conv_branch(u.reshape(B, L, D), gates.reshape(B, L, 2 * D), conv_w, row(conv_b),
                      row(conv_ln_g), row(conv_ln_b), w_conv_out.astype(BF16), 512)

    mask = _index_mask(qi.reshape(B, L, -1), ki.reshape(B, L, IDX_DIM),
                       kw.reshape(B, L, LANES), k_top)
    attn = _sparse_attention(qkv.reshape(B, L, 3 * D), mask)

    h1 = _mix(x2, attn.reshape(T, D), yc.reshape(T, D), gates,
              w_attn_out.astype(BF16), w_mix_out.astype(BF16), tm)

    h2 = _cross_attention(h1.reshape(B, L, D), mem, row(norm_x_g), row(norm_mem_g),
                          wx_q.astype(BF16), wx_kv.astype(BF16), wx_o.astype(BF16), tm)

    out = _ffn(h2.reshape(T, D), row(norm_ffn_g), w_ff1.astype(BF16), w_ff2.astype(BF16),
               final_g, tm, 1024)
    return out.reshape(B, L, D)


def kernel(x, mem, norm_mix_g, w_in, b_gate, conv_w, conv_b, conv_ln_g, conv_ln_b, w_conv_out,
           w_attn_out, w_mix_out, norm_x_g, norm_mem_g, wx_q, wx_kv, wx_o, norm_ffn_g, w_ff1,
           w_ff2, norm_final_g):
    depth = w_in.shape[0]
    assert depth == 1, "final RMSNorm is fused into the last layer's MLP kernel"
    assert x.shape[-1] == D_MODEL
    return _layer(x, mem, norm_mix_g[0], w_in[0], b_gate[0], conv_w[0], conv_b[0], conv_ln_g[0],
                  conv_ln_b[0], w_conv_out[0], w_attn_out[0], w_mix_out[0], norm_x_g[0],
                  norm_mem_g[0], wx_q[0], wx_kv[0], wx_o[0], norm_ffn_g[0], w_ff1[0], w_ff2[0],
                  norm_final_g.reshape(1, -1))
```

```python
import functools
import math

import numpy as np
import jax
import jax.numpy as jnp
from jax import lax
from jax.experimental import pallas as pl
from jax.experimental.pallas import tpu as pltpu

F32 = jnp.float32
BF16 = jnp.bfloat16
I32 = jnp.int32

EPS = 1e-6
D_MODEL = 1024
N_HEADS = 8
HEAD_DIM = 128
IDX_HEADS = 8
IDX_DIM = 64
TOPK_MAX = 256
CONV_WIDTH = 31
X_HEADS = 4
X_HEAD_DIM = 128
X_W = X_HEADS * X_HEAD_DIM
LANES = 128
LOG2E = math.log2(math.e)

INT_MIN = -(2 ** 31)
KEY_NEG_INF = int(np.int32(np.uint32(0xFF800000) ^ np.uint32(0x7FFFFFFF)))
HI_NEG_INF = KEY_NEG_INF >> 16
HI_MIN_NORMAL = 0x0080

VMEM_LIMIT = 56 * 1024 * 1024

_NT = (((1,), (1,)), ((), ()))


def _params(sem):
    return pltpu.CompilerParams(dimension_semantics=sem, vmem_limit_bytes=VMEM_LIMIT)


def _rms_bf16(x, g):
    ms = jnp.mean(x * x, axis=-1, keepdims=True)
    return (x * lax.rsqrt(ms + EPS) * g).astype(BF16)


def _sigmoid(x):
    return 1.0 / (1.0 + jnp.exp(-x))


N_QI = IDX_HEADS * IDX_DIM
OFF_A = 0
OFF_G = OFF_A + D_MODEL
OFF_QKV = OFF_G + D_MODEL
OFF_QI = OFF_QKV + 3 * D_MODEL
OFF_KW = OFF_QI + N_QI
OFF_GATE = OFF_KW + LANES
PROJ_WIDTH = OFF_GATE + 2 * D_MODEL
PROJ_TN = 512


def _proj_kernel(x_ref, g_ref, w_ref, b_ref, u_ref, qkv_ref, qi_ref, ki_ref, kw_ref, gate_ref):
    D = D_MODEL
    xn = _rms_bf16(x_ref[...], g_ref[...])

    def mm(lo, width):
        return jnp.dot(xn, w_ref[:, lo:lo + width], preferred_element_type=F32)

    for c in range(0, D, PROJ_TN):
        u_ref[:, c:c + PROJ_TN] = (mm(OFF_A + c, PROJ_TN) *
                                   _sigmoid(mm(OFF_G + c, PROJ_TN))).astype(BF16)
    for c in range(0, 3 * D, PROJ_TN):
        scale = HEAD_DIM ** -0.5 * LOG2E if c < D else 1.0
        qkv_ref[:, c:c + PROJ_TN] = (mm(OFF_QKV + c, PROJ_TN) * scale).astype(BF16)
    qi_ref[...] = (mm(OFF_QI, N_QI) * (IDX_DIM ** -0.5)).astype(BF16)
    kw = mm(OFF_KW, LANES)
    ki_ref[...] = kw[:, :IDX_DIM].astype(BF16)
    kw_ref[...] = kw * (IDX_HEADS ** -0.5)
    for c in range(0, 2 * D, PROJ_TN):
        gate_ref[:, c:c + PROJ_TN] = _sigmoid(
            mm(OFF_GATE + c, PROJ_TN) + b_ref[:, c:c + PROJ_TN]).astype(BF16)


def _projections(x2, g, w_all, b_gate, tm):
    T, D = x2.shape
    row = lambda i: (i, 0)
    fixed = lambda i: (0, 0)
    widths = (D, 3 * D, N_QI, IDX_DIM, LANES, 2 * D)
    dtypes = (BF16, BF16, BF16, BF16, F32, BF16)
    return pl.pallas_call(
        _proj_kernel,
        grid=(T // tm,),
        in_specs=[pl.BlockSpec((tm, D), row),
                  pl.BlockSpec((1, D), fixed),
                  pl.BlockSpec((D, PROJ_WIDTH), fixed),
                  pl.BlockSpec((1, 2 * D), fixed)],
        out_specs=[pl.BlockSpec((tm, w), row) for w in widths],
        out_shape=[jax.ShapeDtypeStruct((T, w), dt) for w, dt in zip(widths, dtypes)],
        compiler_params=_params(("parallel",)),
        name="proj_in",
    )(x2, g, w_all, b_gate)


CONV_HALO = 32
CONV_ROWS = 64
CONV_LANES = 256


def _conv_kernel(u_ref, halo_ref, cw_ref, cb_ref, lng_ref, lnb_ref, wout_ref, gc_ref,
                 o_ref, win_ref, cv_ref):
    tl = u_ref.shape[1]
    C = u_ref.shape[2]
    i = pl.program_id(1)
    halo = halo_ref[0].astype(F32)
    win_ref[0:CONV_HALO, :] = jnp.where(i > 0, halo, 0.0)
    win_ref[CONV_HALO:CONV_HALO + tl, :] = u_ref[0].astype(F32)
    win_ref[CONV_HALO + tl:, :] = jnp.zeros((8, C), F32)

    s0 = CONV_HALO - (CONV_WIDTH - 1)
    for lc in range(C // CONV_LANES):
        lanes = slice(lc * CONV_LANES, (lc + 1) * CONV_LANES)

        def body(r, carry, lanes=lanes):
            r0 = pl.multiple_of(r * CONV_ROWS, CONV_ROWS)
            out = jnp.zeros((CONV_ROWS, CONV_LANES), F32)
            for res in range(8):
                part = None
                for j in range(CONV_WIDTH):
                    s = j + s0
                    if s % 8 != res:
                        continue
                    a = s - res
                    term = cw_ref[j:j + 1, lanes] * win_ref[pl.ds(r0 + a, CONV_ROWS + 8), lanes]
                    part = term if part is None else part + term
                if part is not None:
                    out = out + part[res:res + CONV_ROWS, :]
            cv_ref[pl.ds(r0, CONV_ROWS), lanes] = out + cb_ref[:, lanes]
            return carry

        lax.fori_loop(0, tl // CONV_ROWS, body, 0)

    y = cv_ref[...]
    mu = jnp.mean(y, axis=-1, keepdims=True)
    yc = y - mu
    var = jnp.mean(yc * yc, axis=-1, keepdims=True)
    z = yc * lax.rsqrt(var + EPS) * lng_ref[...] + lnb_ref[...]
    act = (z * _sigmoid(z)).astype(BF16)
    out = jnp.dot(act, wout_ref[...], preferred_element_type=F32)
    o_ref[0] = (gc_ref[0].astype(F32) * out).astype(BF16)


def _conv_branch(u3, gates3, conv_w, conv_b, ln_g, ln_b, w_out, tl):
    B, L, C = u3.shape
    hb = tl // CONV_HALO
    return pl.pallas_call(
        _conv_kernel,
        grid=(B, L // tl),
        in_specs=[pl.BlockSpec((1, tl, C), lambda b, i: (b, i, 0)),
                  pl.BlockSpec((1, CONV_HALO, C), lambda b, i: (b, jnp.maximum(i * hb - 1, 0), 0)),
                  pl.BlockSpec((CONV_WIDTH, C), lambda b, i: (0, 0)),
                  pl.BlockSpec((1, C), lambda b, i: (0, 0)),
                  pl.BlockSpec((1, C), lambda b, i: (0, 0)),
                  pl.BlockSpec((1, C), lambda b, i: (0, 0)),
                  pl.BlockSpec((C, C), lambda b, i: (0, 0)),
                  pl.BlockSpec((1, tl, C), lambda b, i: (b, i, 0))],
        out_specs=pl.BlockSpec((1, tl, C), lambda b, i: (b, i, 0)),
        out_shape=jax.ShapeDtypeStruct((B, L, C), BF16),
        scratch_shapes=[pltpu.VMEM((tl + CONV_HALO + 8, C), F32), pltpu.VMEM((tl, C), F32)],
        compiler_params=_params(("parallel", "parallel")),
        name="conv_branch",
    )(u3, u3, conv_w, conv_b, ln_g, ln_b, w_out, gates3)


IDX_TQ = 256
IDX_TK = 512


HI_BITS = -65536
IDX_ROWS = 128


def _hi16_as_float(hi):
    pat = hi ^ ((hi >> 31) & 0x7FFF)
    return pltpu.bitcast(pat << 16, F32)


def _index_kernel(qi_ref, ki_ref, kw_ref, mask_ref, key_ref, hi_ref, wb_ref, *, k_top):
    tq = qi_ref.shape[1]
    L = ki_ref.shape[1]
    nkc = L // IDX_TK
    ngrp = IDX_TK // LANES
    i = pl.program_id(1)
    q0 = i * tq
    nk = (q0 + tq + IDX_TK - 1) // IDX_TK
    kf = float(k_top)

    w = kw_ref[0][:, IDX_DIM:IDX_DIM + IDX_HEADS]
    for h in range(IDX_HEADS):
        wb_ref[h] = jnp.broadcast_to(w[:, h:h + 1], (tq, LANES))

    qpos = q0 + lax.broadcasted_iota(I32, (tq, IDX_TK), 0)
    lane_iota = lax.broadcasted_iota(I32, (tq, IDX_TK), 1)

    def score_chunk(c, diagonal):
        k0 = pl.multiple_of(c * IDX_TK, IDX_TK)
        kc = ki_ref[0, pl.ds(k0, IDX_TK), :]
        score = jnp.zeros((tq, IDX_TK), F32)
        for h in range(IDX_HEADS):
            qh = qi_ref[0, :, h * IDX_DIM:(h + 1) * IDX_DIM]
            logit = lax.dot_general(qh, kc, _NT, preferred_element_type=F32)
            wbh = wb_ref[h]
            score = score + jnp.concatenate([wbh] * ngrp, axis=1) * jnp.maximum(logit, 0.0)
        bits = pltpu.bitcast(score, I32)
        bits = jnp.where((bits & 0x7F800000) == 0, 0, bits)
        key = bits ^ ((bits >> 31) & 0x7FFFFFFF)
        hi_bits = bits & HI_BITS
        if diagonal:
            causal = k0 + lane_iota <= qpos
            key = jnp.where(causal, key, INT_MIN)
            hi_bits = jnp.where(causal, hi_bits, HI_BITS)
        key_ref[c] = key
        hi_ref[c] = pltpu.bitcast(hi_bits, F32).astype(BF16)

    def score_body(c, carry):
        score_chunk(c, False)
        return carry

    lax.fori_loop(0, nk - 1, score_body, 0)
    score_chunk(nk - 1, True)

    ones_mat = jnp.ones((LANES, LANES), BF16)

    def rowsum(acc):
        return jnp.dot(acc.astype(BF16), ones_mat, preferred_element_type=F32)

    def count_hi(cand):
        one = jnp.ones((tq, LANES), BF16)
        zero = jnp.zeros((tq, LANES), BF16)

        def body(c, acc):
            hc = hi_ref[c]
            for g in range(ngrp):
                acc = acc + jnp.where(hc[:, g * LANES:(g + 1) * LANES] >= cand, one, zero)
            return acc
        return rowsum(lax.fori_loop(0, nk, body, zero))

    def count_key(pred, *row_args):
        parts = []
        for r0 in range(0, tq, IDX_ROWS):
            rows = slice(r0, r0 + IDX_ROWS)
            args = [a[rows] for a in row_args]
            pos = lax.broadcasted_iota(I32, (IDX_ROWS, LANES), 1)

            def body(c, acc, rows=rows, args=args, pos=pos):
                for g in range(ngrp):
                    kg = key_ref[c, rows, g * LANES:(g + 1) * LANES]
                    m = pred(kg, c * IDX_TK + g * LANES + pos, *args)
                    acc = acc + jnp.where(m, 1.0, 0.0)
                return acc
            parts.append(lax.fori_loop(0, nk, body, jnp.zeros((IDX_ROWS, LANES), F32)))
        return rowsum(jnp.concatenate(parts, axis=0))

    def hi_body(it, carry):
        t_hi, cnt_t = carry
        cand_u = t_hi | lax.shift_left(jnp.int32(1), 15 - it)
        cand_s = cand_u - 32768
        cand_s = jnp.where((cand_s >= 1) & (cand_s < HI_MIN_NORMAL), HI_MIN_NORMAL, cand_s)
        cand_s = jnp.where((cand_s >= -HI_MIN_NORMAL) & (cand_s <= -1), 0, cand_s)
        cand = jnp.where(cand_s < HI_NEG_INF, -jnp.inf, _hi16_as_float(cand_s)).astype(BF16)
        cnt = count_hi(cand)
        take = cnt >= kf
        return jnp.where(take, cand_u, t_hi), jnp.where(take, cnt, cnt_t)

    t_hi, cnt_t = lax.fori_loop(
        0, 16, hi_body, (jnp.zeros((tq, LANES), I32), jnp.zeros((tq, LANES), F32)))

    zero_bucket = t_hi == 0x8000

    def unsettled(cnt):
        return jnp.max(jnp.where((cnt > kf) & jnp.logical_not(zero_bucket), 1.0, 0.0)) > 0.0

    def lo_cond(carry):
        it, _, _, go = carry
        return jnp.logical_and(it < 16, go)

    def lo_body(carry):
        it, t_u, cnt_t, _ = carry
        cand_u = t_u | lax.shift_left(jnp.int32(1), 15 - it)
        cand_s = cand_u ^ INT_MIN
        cnt = count_key(lambda kg, kpos, cs: kg >= cs, cand_s)
        take = cnt >= kf
        cnt_t = jnp.where(take, cnt, cnt_t)
        return it + 1, jnp.where(take, cand_u, t_u), cnt_t, unsettled(cnt_t)

    _, t_u, cnt_t, _ = lax.while_loop(
        lo_cond, lo_body, (jnp.int32(0), t_hi << 16, cnt_t, unsettled(cnt_t)))

    t_s = jnp.maximum(t_u ^ INT_MIN, INT_MIN + 1)
    tied = cnt_t > kf
    neg_inf = jnp.float32(-jnp.inf)

    def write_plain():
        def body(c, carry):
            k0 = pl.multiple_of(c * IDX_TK, IDX_TK)
            kc = key_ref[c]
            cols = [jnp.where(kc[:, g * LANES:(g + 1) * LANES] >= t_s, 0.0, neg_inf)
                    for g in range(ngrp)]
            mask_ref[0, :, pl.ds(k0, IDX_TK)] = jnp.concatenate(cols, axis=1).astype(BF16)
            return carry
        lax.fori_loop(0, nk, body, 0)

    def write_tied():
        need = kf - count_key(lambda kg, kpos, ts: kg > ts, t_s)
        ri = lax.broadcasted_iota(I32, (IDX_TK, IDX_TK + LANES), 0)
        ci = lax.broadcasted_iota(I32, (IDX_TK, IDX_TK + LANES), 1)
        upper = jnp.where((ri <= ci) | (ci >= IDX_TK), 1.0, 0.0).astype(BF16)
        need_w = jnp.concatenate([need] * ngrp, axis=1)
        t_w = jnp.concatenate([t_s] * ngrp, axis=1)

        def body(c, before):
            k0 = pl.multiple_of(c * IDX_TK, IDX_TK)
            kc = key_ref[c]
            tie = kc == t_w
            pref = jnp.dot(jnp.where(tie, 1.0, 0.0).astype(BF16), upper,
                           preferred_element_type=F32)
            rank = pref[:, :IDX_TK] + jnp.concatenate([before] * ngrp, axis=1)
            sel = (kc > t_w) | (tie & (rank <= need_w))
            mask_ref[0, :, pl.ds(k0, IDX_TK)] = jnp.where(sel, 0.0, neg_inf).astype(BF16)
            return before + pref[:, IDX_TK:]
        lax.fori_loop(0, nk, body, jnp.zeros((tq, LANES), F32))

    any_tied = jnp.max(jnp.where(tied, 1.0, 0.0)) > 0.0
    lax.cond(any_tied, write_tied, write_plain)

    def fill_body(c, carry):
        k0 = pl.multiple_of(c * IDX_TK, IDX_TK)
        mask_ref[0, :, pl.ds(k0, IDX_TK)] = jnp.full((tq, IDX_TK), -jnp.inf, BF16)
        return carry

    lax.fori_loop(nk, nkc, fill_body, 0)


def _index_mask(qi3, ki3, kw3, k_top):
    B, L, nqi = qi3.shape
    tq = IDX_TQ
    assert IDX_TK % tq == 0 and L % IDX_TK == 0 and tq % IDX_ROWS == 0
    assert L // LANES <= 256, "per-lane counts must stay exact in bf16"
    return pl.pallas_call(
        functools.partial(_index_kernel, k_top=k_top),
        grid=(B, L // tq),
        in_specs=[pl.BlockSpec((1, tq, nqi), lambda b, i: (b, i, 0)),
                  pl.BlockSpec((1, L, IDX_DIM), lambda b, i: (b, 0, 0)),
                  pl.BlockSpec((1, tq, LANES), lambda b, i: (b, i, 0))],
        out_specs=pl.BlockSpec((1, tq, L), lambda b, i: (b, i, 0)),
        out_shape=jax.ShapeDtypeStruct((B, L, L), BF16),
        scratch_shapes=[pltpu.VMEM((L // IDX_TK, tq, IDX_TK), I32),
                        pltpu.VMEM((L // IDX_TK, tq, IDX_TK), BF16),
                        pltpu.VMEM((IDX_HEADS, tq, LANES), F32)],
        compiler_params=_params(("parallel", "parallel")),
        name="index_mask",
    )(qi3, ki3, kw3)


ATT_TQ = 512
ATT_TK = 1024


def _attn_kernel(q_ref, k_ref, v_ref, m_ref, o_ref, acc_ref, mx_ref):
    tq = q_ref.shape[1]
    tk = k_ref.shape[1]
    ngrp = tk // LANES
    i = pl.program_id(1)
    kb = pl.program_id(2)
    last = ((i + 1) * tq - 1) // tk

    @pl.when(kb == 0)
    def _():
        acc_ref[...] = jnp.zeros_like(acc_ref)
        mx_ref[...] = jnp.full_like(mx_ref, -jnp.inf)

    @pl.when(kb <= last)
    def _():
        bias = m_ref[0].astype(F32)
        ones = jnp.ones((tk, HEAD_DIM), BF16)
        for h in range(N_HEADS):
            hs = slice(h * HEAD_DIM, (h + 1) * HEAD_DIM)
            s = lax.dot_general(q_ref[0, :, hs], k_ref[0, :, hs], _NT,
                                preferred_element_type=F32) + bias
            m_old = mx_ref[h]
            m_new = jnp.maximum(m_old, jnp.max(s, axis=1, keepdims=True))
            m_safe = jnp.where(m_new == -jnp.inf, 0.0, m_new)
            alpha = jnp.exp2(m_old - m_safe)
            p = jnp.exp2(s - jnp.concatenate([m_safe] * ngrp, axis=1)).astype(BF16)
            v_ext = jnp.concatenate([v_ref[0, :, hs], ones], axis=1)
            acc_ref[h] = jnp.concatenate([alpha, alpha], axis=1) * acc_ref[h] + jnp.dot(
                p, v_ext, preferred_element_type=F32)
            mx_ref[h] = m_new

    @pl.when(kb == last)
    def _():
        for h in range(N_HEADS):
            hs = slice(h * HEAD_DIM, (h + 1) * HEAD_DIM)
            a = acc_ref[h]
            o_ref[0, :, hs] = (a[:, :HEAD_DIM] / a[:, HEAD_DIM:]).astype(BF16)


def _sparse_attention(qkv3, mask):
    B, L, _ = qkv3.shape
    W = N_HEADS * HEAD_DIM
    tq, tk = ATT_TQ, ATT_TK

    def kclamp(i, kb):
        return jnp.minimum(kb, ((i + 1) * tq - 1) // tk)

    return pl.pallas_call(
        _attn_kernel,
        grid=(B, L // tq, L // tk),
        in_specs=[pl.BlockSpec((1, tq, W), lambda b, i, kb: (b, i, 0)),
                  pl.BlockSpec((1, tk, W), lambda b, i, kb: (b, kclamp(i, kb), 1)),
                  pl.BlockSpec((1, tk, W), lambda b, i, kb: (b, kclamp(i, kb), 2)),
                  pl.BlockSpec((1, tq, tk), lambda b, i, kb: (b, i, kclamp(i, kb)))],
        out_specs=pl.BlockSpec((1, tq, W), lambda b, i, kb: (b, i, 0)),
        out_shape=jax.ShapeDtypeStruct((B, L, W), BF16),
        scratch_shapes=[pltpu.VMEM((N_HEADS, tq, 2 * HEAD_DIM), F32),
                        pltpu.VMEM((N_HEADS, tq, LANES), F32)],
        compiler_params=_params(("parallel", "parallel", "arbitrary")),
        name="sparse_attn",
    )(qkv3, qkv3, qkv3, mask)


def _mix_kernel(x_ref, attn_ref, yc_ref, ga_ref, wao_ref, wmo_ref, o_ref):
    ya = jnp.dot(attn_ref[...], wao_ref[...], preferred_element_type=F32)
    m = yc_ref[...].astype(F32) + ga_ref[...].astype(F32) * ya
    o_ref[...] = x_ref[...] + jnp.dot(m.astype(BF16), wmo_ref[...], preferred_element_type=F32)


def _mix(x2, attn2, yc2, gates2, w_ao, w_mo, tm):
    T, D = x2.shape
    row = lambda i: (i, 0)
    return pl.pallas_call(
        _mix_kernel,
        grid=(T // tm,),
        in_specs=[pl.BlockSpec((tm, D), row),
                  pl.BlockSpec((tm, D), row),
                  pl.BlockSpec((tm, D), row),
                  pl.BlockSpec((tm, D), lambda i: (i, 1)),
                  pl.BlockSpec((D, D), lambda i: (0, 0)),
                  pl.BlockSpec((D, D), lambda i: (0, 0))],
        out_specs=pl.BlockSpec((tm, D), row),
        out_shape=jax.ShapeDtypeStruct((T, D), F32),
        compiler_params=_params(("parallel",)),
        name="mix_out",
    )(x2, attn2, yc2, gates2, w_ao, w_mo)


def _memkv_kernel(mem_ref, g_ref, w_ref, o_ref):
    mn = _rms_bf16(mem_ref[0], g_ref[...])
    o_ref[0] = jnp.dot(mn, w_ref[...], preferred_element_type=F32).astype(BF16)


def _xattn_kernel(h_ref, g_ref, wq_ref, kv_ref, wo_ref, o_ref):
    h = h_ref[0]
    hn = _rms_bf16(h, g_ref[...])
    q = (jnp.dot(hn, wq_ref[...], preferred_element_type=F32) * (X_HEAD_DIM ** -0.5)).astype(BF16)
    outs = []
    for hh in range(X_HEADS):
        hs = slice(hh * X_HEAD_DIM, (hh + 1) * X_HEAD_DIM)
        vs = slice(X_W + hh * X_HEAD_DIM, X_W + (hh + 1) * X_HEAD_DIM)
        s = lax.dot_general(q[:, hs], kv_ref[0, :, hs], _NT, preferred_element_type=F32)
        p = jnp.exp(s - jnp.max(s, axis=1, keepdims=True))
        l = jnp.sum(p, axis=1, keepdims=True)
        o = jnp.dot(p.astype(BF16), kv_ref[0, :, vs], preferred_element_type=F32) / l
        outs.append(o.astype(BF16))
    o = jnp.concatenate(outs, axis=1)
    o_ref[0] = h + jnp.dot(o, wo_ref[...], preferred_element_type=F32)


def _cross_attention(h3, mem, g_x, g_mem, wq, wkv, wo, tm):
    B, L, D = h3.shape
    n_mem = mem.shape[1]
    kv = pl.pallas_call(
        _memkv_kernel,
        grid=(B,),
        in_specs=[pl.BlockSpec((1, n_mem, D), lambda b: (b, 0, 0)),
                  pl.BlockSpec((1, D), lambda b: (0, 0)),
                  pl.BlockSpec((D, 2 * X_W), lambda b: (0, 0))],
        out_specs=pl.BlockSpec((1, n_mem, 2 * X_W), lambda b: (b, 0, 0)),
        out_shape=jax.ShapeDtypeStruct((B, n_mem, 2 * X_W), BF16),
        compiler_params=_params(("parallel",)),
        name="mem_kv",
    )(mem, g_mem, wkv)

    return pl.pallas_call(
        _xattn_kernel,
        grid=(B, L // tm),
        in_specs=[pl.BlockSpec((1, tm, D), lambda b, i: (b, i, 0)),
                  pl.BlockSpec((1, D), lambda b, i: (0, 0)),
                  pl.BlockSpec((D, X_W), lambda b, i: (0, 0)),
                  pl.BlockSpec((1, n_mem, 2 * X_W), lambda b, i: (b, 0, 0)),
                  pl.BlockSpec((X_W, D), lambda b, i: (0, 0))],
        out_specs=pl.BlockSpec((1, tm, D), lambda b, i: (b, i, 0)),
        out_shape=jax.ShapeDtypeStruct((B, L, D), F32),
        compiler_params=_params(("parallel", "parallel")),
        name="cross_attn",
    )(h3, g_x, wq, kv, wo)


def _ffn_kernel(h_ref, g_ref, w1_ref, w2_ref, gf_ref, o_ref, hn_ref, acc_ref):
    j = pl.program_id(1)

    @pl.when(j == 0)
    def _():
        h = h_ref[...]
        hn_ref[...] = _rms_bf16(h, g_ref[...])
        acc_ref[...] = h

    a = jnp.dot(hn_ref[...], w1_ref[...], preferred_element_type=F32)
    r = jnp.maximum(a, 0.0)
    acc_ref[...] += jnp.dot((r * r).astype(BF16), w2_ref[...], preferred_element_type=F32)

    @pl.when(j == pl.num_programs(1) - 1)
    def _():
        y = acc_ref[...]
        ms = jnp.mean(y * y, axis=-1, keepdims=True)
        o_ref[...] = y * lax.rsqrt(ms + EPS) * gf_ref[...]


def _ffn(h2, g, w1, w2, g_final, tm, tf):
    T, D = h2.shape
    F = w1.shape[1]
    return pl.pallas_call(
        _ffn_kernel,
        grid=(T // tm, F // tf),
        in_specs=[pl.BlockSpec((tm, D), lambda i, j: (i, 0)),
                  pl.BlockSpec((1, D), lambda i, j: (0, 0)),
                  pl.BlockSpec((D, tf), lambda i, j: (0, j)),
                  pl.BlockSpec((tf, D), lambda i, j: (j, 0)),
                  pl.BlockSpec((1, D), lambda i, j: (0, 0))],
        out_specs=pl.BlockSpec((tm, D), lambda i, j: (i, 0)),
        out_shape=jax.ShapeDtypeStruct((T, D), F32),
        scratch_shapes=[pltpu.VMEM((tm, D), BF16), pltpu.VMEM((tm, D), F32)],
        compiler_params=_params(("parallel", "arbitrary")),
        name="ffn",
    )(h2, g, w1, w2, g_final)


def _layer(h3, mem, norm_mix_g, w_in, b_gate, conv_w, conv_b, conv_ln_g, conv_ln_b,
           w_conv_out, w_attn_out, w_mix_out, norm_x_g, norm_mem_g, wx_q, wx_kv, wx_o,
           norm_ffn_g, w_ff1, w_ff2, final_g):
    B, L, D = h3.shape
    T = B * L
    tm = 512
    k_top = min(TOPK_MAX, L // 4)
    row = lambda v: v.reshape(1, -1)

    n_kw = IDX_DIM + IDX_HEADS
    w_all = jnp.concatenate(
        [w_in[:, :OFF_KW], jnp.pad(w_in[:, OFF_KW:OFF_KW + n_kw], ((0, 0), (0, LANES - n_kw))),
         w_in[:, OFF_KW + n_kw:]], axis=1).astype(BF16)
    assert w_all.shape[1] == PROJ_WIDTH

    x2 = h3.reshape(T, D)
    u, qkv, qi, ki, kw, gates = _projections(x2, row(norm_mix_g), w_all, row(b_gate), tm)

    yc = _conv_branch(u.reshape(B, L, D), gates.reshape(B, L, 2 * D), conv_w, row(conv_b),
                      row(conv_ln_g), row(conv_ln_b), w_conv_out.astype(BF16), 512)

    mask = _index_mask(qi.reshape(B, L, -1), ki.reshape(B, L, IDX_DIM),
                       kw.reshape(B, L, LANES), k_top)
    attn = _sparse_attention(qkv.reshape(B, L, 3 * D), mask)

    h1 = _mix(x2, attn.reshape(T, D), yc.reshape(T, D), gates,
              w_attn_out.astype(BF16), w_mix_out.astype(BF16), tm)

    h2 = _cross_attention(h1.reshape(B, L, D), mem, row(norm_x_g), row(norm_mem_g),
                          wx_q.astype(BF16), wx_kv.astype(BF16), wx_o.astype(BF16), tm)

    out = _ffn(h2.reshape(T, D), row(norm_ffn_g), w_ff1.astype(BF16), w_ff2.astype(BF16),
               final_g, 1024, 1024)
    return out.reshape(B, L, D)


def kernel(x, mem, norm_mix_g, w_in, b_gate, conv_w, conv_b, conv_ln_g, conv_ln_b, w_conv_out,
           w_attn_out, w_mix_out, norm_x_g, norm_mem_g, wx_q, wx_kv, wx_o, norm_ffn_g, w_ff1,
           w_ff2, norm_final_g):
    depth = w_in.shape[0]
    assert depth == 1, "final RMSNorm is fused into the last layer's MLP kernel"
    assert x.shape[-1] == D_MODEL
    return _layer(x, mem, norm_mix_g[0], w_in[0], b_gate[0], conv_w[0], conv_b[0], conv_ln_g[0],
                  conv_ln_b[0], w_conv_out[0], w_attn_out[0], w_mix_out[0], norm_x_g[0],
                  norm_mem_g[0], wx_q[0], wx_kv[0], wx_o[0], norm_ffn_g[0], w_ff1[0], w_ff2[0],
                  norm_final_g.reshape(1, -1))
```

```python
import functools
import math

import numpy as np
import jax
import jax.numpy as jnp
from jax import lax
from jax.experimental import pallas as pl
from jax.experimental.pallas import tpu as pltpu

F32 = jnp.float32
BF16 = jnp.bfloat16
I32 = jnp.int32

EPS = 1e-6
D_MODEL = 1024
N_HEADS = 8
HEAD_DIM = 128
IDX_HEADS = 8
IDX_DIM = 64
TOPK_MAX = 256
CONV_WIDTH = 31
X_HEADS = 4
X_HEAD_DIM = 128
X_W = X_HEADS * X_HEAD_DIM
LANES = 128
LOG2E = math.log2(math.e)

INT_MIN = -(2 ** 31)
KEY_NEG_INF = int(np.int32(np.uint32(0xFF800000) ^ np.uint32(0x7FFFFFFF)))
HI_NEG_INF = KEY_NEG_INF >> 16
HI_MIN_NORMAL = 0x0080

VMEM_LIMIT = 56 * 1024 * 1024

_NT = (((1,), (1,)), ((), ()))


def _params(sem):
    return pltpu.CompilerParams(dimension_semantics=sem, vmem_limit_bytes=VMEM_LIMIT)


def _rms_bf16(x, g):
    ms = jnp.mean(x * x, axis=-1, keepdims=True)
    return (x * lax.rsqrt(ms + EPS) * g).astype(BF16)


def _sigmoid(x):
    return 1.0 / (1.0 + jnp.exp(-x))


N_QI = IDX_HEADS * IDX_DIM
OFF_A = 0
OFF_G = OFF_A + D_MODEL
OFF_QKV = OFF_G + D_MODEL
OFF_QI = OFF_QKV + 3 * D_MODEL
OFF_KW = OFF_QI + N_QI
OFF_GATE = OFF_KW + LANES
PROJ_WIDTH = OFF_GATE + 2 * D_MODEL
PROJ_TN = 512


def _proj_kernel(x_ref, g_ref, w_ref, b_ref, u_ref, qkv_ref, qi_ref, ki_ref, kw_ref, gate_ref):
    D = D_MODEL
    xn = _rms_bf16(x_ref[...], g_ref[...])

    def mm(lo, width):
        return jnp.dot(xn, w_ref[:, lo:lo + width], preferred_element_type=F32)

    for c in range(0, D, PROJ_TN):
        u_ref[:, c:c + PROJ_TN] = (mm(OFF_A + c, PROJ_TN) *
                                   _sigmoid(mm(OFF_G + c, PROJ_TN))).astype(BF16)
    for c in range(0, 3 * D, PROJ_TN):
        scale = HEAD_DIM ** -0.5 * LOG2E if c < D else 1.0
        qkv_ref[:, c:c + PROJ_TN] = (mm(OFF_QKV + c, PROJ_TN) * scale).astype(BF16)
    qi_ref[...] = (mm(OFF_QI, N_QI) * (IDX_DIM ** -0.5)).astype(BF16)
    kw = mm(OFF_KW, LANES)
    ki_ref[...] = kw[:, :IDX_DIM].astype(BF16)
    kw_ref[...] = kw * (IDX_HEADS ** -0.5)
    for c in range(0, 2 * D, PROJ_TN):
        gate_ref[:, c:c + PROJ_TN] = _sigmoid(
            mm(OFF_GATE + c, PROJ_TN) + b_ref[:, c:c + PROJ_TN]).astype(BF16)


def _projections(x2, g, w_all, b_gate, tm):
    T, D = x2.shape
    row = lambda i: (i, 0)
    fixed = lambda i: (0, 0)
    widths = (D, 3 * D, N_QI, IDX_DIM, LANES, 2 * D)
    dtypes = (BF16, BF16, BF16, BF16, F32, BF16)
    return pl.pallas_call(
        _proj_kernel,
        grid=(T // tm,),
        in_specs=[pl.BlockSpec((tm, D), row),
                  pl.BlockSpec((1, D), fixed),
                  pl.BlockSpec((D, PROJ_WIDTH), fixed),
                  pl.BlockSpec((1, 2 * D), fixed)],
        out_specs=[pl.BlockSpec((tm, w), row) for w in widths],
        out_shape=[jax.ShapeDtypeStruct((T, w), dt) for w, dt in zip(widths, dtypes)],
        compiler_params=_params(("parallel",)),
        name="proj_in",
    )(x2, g, w_all, b_gate)


CONV_HALO = 32
CONV_ROWS = 64
CONV_LANES = 256


def _conv_kernel(u_ref, halo_ref, cw_ref, cb_ref, lng_ref, lnb_ref, wout_ref, gc_ref,
                 o_ref, win_ref, cv_ref):
    tl = u_ref.shape[1]
    C = u_ref.shape[2]
    i = pl.program_id(1)
    halo = halo_ref[0].astype(F32)
    win_ref[0:CONV_HALO, :] = jnp.where(i > 0, halo, 0.0)
    win_ref[CONV_HALO:CONV_HALO + tl, :] = u_ref[0].astype(F32)
    win_ref[CONV_HALO + tl:, :] = jnp.zeros((8, C), F32)

    s0 = CONV_HALO - (CONV_WIDTH - 1)
    for lc in range(C // CONV_LANES):
        lanes = slice(lc * CONV_LANES, (lc + 1) * CONV_LANES)

        def body(r, carry, lanes=lanes):
            r0 = pl.multiple_of(r * CONV_ROWS, CONV_ROWS)
            out = jnp.zeros((CONV_ROWS, CONV_LANES), F32)
            for res in range(8):
                part = None
                for j in range(CONV_WIDTH):
                    s = j + s0
                    if s % 8 != res:
                        continue
                    a = s - res
                    term = cw_ref[j:j + 1, lanes] * win_ref[pl.ds(r0 + a, CONV_ROWS + 8), lanes]
                    part = term if part is None else part + term
                if part is not None:
                    out = out + part[res:res + CONV_ROWS, :]
            cv_ref[pl.ds(r0, CONV_ROWS), lanes] = out + cb_ref[:, lanes]
            return carry

        lax.fori_loop(0, tl // CONV_ROWS, body, 0)

    y = cv_ref[...]
    mu = jnp.mean(y, axis=-1, keepdims=True)
    yc = y - mu
    var = jnp.mean(yc * yc, axis=-1, keepdims=True)
    z = yc * lax.rsqrt(var + EPS) * lng_ref[...] + lnb_ref[...]
    act = (z * _sigmoid(z)).astype(BF16)
    out = jnp.dot(act, wout_ref[...], preferred_element_type=F32)
    o_ref[0] = (gc_ref[0].astype(F32) * out).astype(BF16)


def _conv_branch(u3, gates3, conv_w, conv_b, ln_g, ln_b, w_out, tl):
    B, L, C = u3.shape
    hb = tl // CONV_HALO
    return pl.pallas_call(
        _conv_kernel,
        grid=(B, L // tl),
        in_specs=[pl.BlockSpec((1, tl, C), lambda b, i: (b, i, 0)),
                  pl.BlockSpec((1, CONV_HALO, C), lambda b, i: (b, jnp.maximum(i * hb - 1, 0), 0)),
                  pl.BlockSpec((CONV_WIDTH, C), lambda b, i: (0, 0)),
                  pl.BlockSpec((1, C), lambda b, i: (0, 0)),
                  pl.BlockSpec((1, C), lambda b, i: (0, 0)),
                  pl.BlockSpec((1, C), lambda b, i: (0, 0)),
                  pl.BlockSpec((C, C), lambda b, i: (0, 0)),
                  pl.BlockSpec((1, tl, C), lambda b, i: (b, i, 0))],
        out_specs=pl.BlockSpec((1, tl, C), lambda b, i: (b, i, 0)),
        out_shape=jax.ShapeDtypeStruct((B, L, C), BF16),
        scratch_shapes=[pltpu.VMEM((tl + CONV_HALO + 8, C), F32), pltpu.VMEM((tl, C), F32)],
        compiler_params=_params(("parallel", "parallel")),
        name="conv_branch",
    )(u3, u3, conv_w, conv_b, ln_g, ln_b, w_out, gates3)


IDX_TQ = 256
IDX_TK = 512


HI_BITS = -65536


def _hi16_as_float(hi):
    pat = hi ^ ((hi >> 31) & 0x7FFF)
    return pltpu.bitcast(pat << 16, F32)


HI_ACC_ROWS = 64
KEY_ACC_ROWS = 32


def _index_kernel(qi_ref, ki_ref, w_ref, mask_ref, key_ref, hi_ref, *, k_top):
    tq = qi_ref.shape[1]
    L = ki_ref.shape[1]
    nkc = L // IDX_TK
    i = pl.program_id(1)
    q0 = i * tq
    nk = (q0 + tq + IDX_TK - 1) // IDX_TK
    kf = float(k_top)

    def score_chunk(c, diagonal):
        k0 = pl.multiple_of(c * IDX_TK, IDX_TK)
        kc = ki_ref[0, pl.ds(k0, IDX_TK), :]
        score = jnp.zeros((IDX_TK, tq), F32)
        for h in range(IDX_HEADS):
            qh = qi_ref[0, :, h * IDX_DIM:(h + 1) * IDX_DIM]
            logit = lax.dot_general(kc, qh, _NT, preferred_element_type=F32)
            score = score + w_ref[0, h:h + 1, :] * jnp.maximum(logit, 0.0)
        bits = pltpu.bitcast(score, I32)
        bits = jnp.where((bits & 0x7F800000) == 0, 0, bits)
        key = bits ^ ((bits >> 31) & 0x7FFFFFFF)
        hi_bits = bits & HI_BITS
        if diagonal:
            kpos = k0 + lax.broadcasted_iota(I32, (IDX_TK, tq), 0)
            qpos = q0 + lax.broadcasted_iota(I32, (IDX_TK, tq), 1)
            causal = kpos <= qpos
            key = jnp.where(causal, key, INT_MIN)
            hi_bits = jnp.where(causal, hi_bits, HI_BITS)
        key_ref[c] = key
        hi_ref[c] = pltpu.bitcast(hi_bits, F32).astype(BF16)

    def score_body(c, carry):
        score_chunk(c, False)
        return carry

    lax.fori_loop(0, nk - 1, score_body, 0)
    score_chunk(nk - 1, True)

    def colsum(acc):
        return jnp.sum(acc.astype(F32), axis=0, keepdims=True)

    def count_hi(cand):
        cb = jnp.broadcast_to(cand, (HI_ACC_ROWS, tq))
        one = jnp.ones((HI_ACC_ROWS, tq), BF16)
        zero = jnp.zeros((HI_ACC_ROWS, tq), BF16)

        def body(c, acc):
            for j in range(0, IDX_TK, HI_ACC_ROWS):
                acc = acc + jnp.where(hi_ref[c, j:j + HI_ACC_ROWS, :] >= cb, one, zero)
            return acc
        return colsum(lax.fori_loop(0, nk, body, zero))

    def count_key(pred, *col_args):
        args = [jnp.broadcast_to(a, (KEY_ACC_ROWS, tq)) for a in col_args]

        def body(c, acc):
            for j in range(0, IDX_TK, KEY_ACC_ROWS):
                m = pred(key_ref[c, j:j + KEY_ACC_ROWS, :], *args)
                acc = acc + jnp.where(m, 1.0, 0.0)
            return acc
        return colsum(lax.fori_loop(0, nk, body, jnp.zeros((KEY_ACC_ROWS, tq), F32)))

    def hi_body(it, carry):
        t_hi, cnt_t = carry
        cand_u = t_hi | lax.shift_left(jnp.int32(1), 15 - it)
        cand_s = cand_u - 32768
        cand_s = jnp.where((cand_s >= 1) & (cand_s < HI_MIN_NORMAL), HI_MIN_NORMAL, cand_s)
        cand_s = jnp.where((cand_s >= -HI_MIN_NORMAL) & (cand_s <= -1), 0, cand_s)
        cand = jnp.where(cand_s < HI_NEG_INF, -jnp.inf, _hi16_as_float(cand_s)).astype(BF16)
        cnt = count_hi(cand)
        take = cnt >= kf
        return jnp.where(take, cand_u, t_hi), jnp.where(take, cnt, cnt_t)

    t_hi, cnt_t = lax.fori_loop(
        0, 16, hi_body, (jnp.zeros((1, tq), I32), jnp.zeros((1, tq), F32)))

    zero_bucket = t_hi == 0x8000

    def unsettled(cnt):
        return jnp.max(jnp.where((cnt > kf) & jnp.logical_not(zero_bucket), 1.0, 0.0)) > 0.0

    def lo_step(it, t_u, cnt_t):
        cand_u = t_u | lax.shift_left(jnp.int32(1), 15 - it)
        cnt = count_key(lambda blk, cs: blk >= cs, cand_u ^ INT_MIN)
        take = cnt >= kf
        return jnp.where(take, cand_u, t_u), jnp.where(take, cnt, cnt_t)

    def lo_cond(carry):
        it, _, _, go = carry
        return jnp.logical_and(it < 16, go)

    def lo_body(carry):
        it, t_u, cnt_t, _ = carry
        t_u, cnt_t = lo_step(it, t_u, cnt_t)
        t_u, cnt_t = lo_step(it + 1, t_u, cnt_t)
        return it + 2, t_u, cnt_t, unsettled(cnt_t)

    _, t_u, cnt_t, _ = lax.while_loop(
        lo_cond, lo_body, (jnp.int32(0), t_hi << 16, cnt_t, unsettled(cnt_t)))

    t_s = jnp.maximum(t_u ^ INT_MIN, INT_MIN + 1)
    tied = cnt_t > kf
    t_w = jnp.broadcast_to(t_s, (IDX_TK, tq))

    def emit(c, sel):
        k0 = pl.multiple_of(c * IDX_TK, IDX_TK)
        blk = jnp.where(sel, 0.0, -jnp.inf).astype(F32).T
        mask_ref[0, :, pl.ds(k0, IDX_TK)] = blk.astype(BF16)

    def write_plain():
        def body(c, carry):
            emit(c, key_ref[c] >= t_w)
            return carry
        lax.fori_loop(0, nk, body, 0)

    def write_tied():
        need = kf - count_key(lambda blk, ts: blk > ts, t_s)
        need_w = jnp.broadcast_to(need, (IDX_TK, tq))
        lower = jnp.where(lax.broadcasted_iota(I32, (IDX_TK, IDX_TK), 1) <=
                          lax.broadcasted_iota(I32, (IDX_TK, IDX_TK), 0), 1.0, 0.0).astype(BF16)

        def body(c, before):
            kc = key_ref[c]
            tie = kc == t_w
            pref = jnp.dot(lower, jnp.where(tie, 1.0, 0.0).astype(BF16),
                           preferred_element_type=F32)
            sel = (kc > t_w) | (tie & (pref + before <= need_w))
            emit(c, sel)
            return before + pref[IDX_TK - 1:IDX_TK, :]
        lax.fori_loop(0, nk, body, jnp.zeros((1, tq), F32))

    any_tied = jnp.max(jnp.where(tied, 1.0, 0.0)) > 0.0
    lax.cond(any_tied, write_tied, write_plain)

    def fill_body(c, carry):
        k0 = pl.multiple_of(c * IDX_TK, IDX_TK)
        mask_ref[0, :, pl.ds(k0, IDX_TK)] = jnp.full((tq, IDX_TK), -jnp.inf, BF16)
        return carry

    lax.fori_loop(nk, nkc, fill_body, 0)


def _index_mask(qi3, ki3, wt3, k_top):
    B, L, nqi = qi3.shape
    tq = IDX_TQ
    assert IDX_TK % tq == 0 and L % IDX_TK == 0
    assert (L // IDX_TK) * (IDX_TK // HI_ACC_ROWS) <= 256, "slot counts must stay exact in bf16"
    return pl.pallas_call(
        functools.partial(_index_kernel, k_top=k_top),
        grid=(B, L // tq),
        in_specs=[pl.BlockSpec((1, tq, nqi), lambda b, i: (b, i, 0)),
                  pl.BlockSpec((1, L, IDX_DIM), lambda b, i: (b, 0, 0)),
                  pl.BlockSpec((1, IDX_HEADS, tq), lambda b, i: (b, 0, i))],
        out_specs=pl.BlockSpec((1, tq, L), lambda b, i: (b, i, 0)),
        out_shape=jax.ShapeDtypeStruct((B, L, L), BF16),
        scratch_shapes=[pltpu.VMEM((L // IDX_TK, IDX_TK, tq), I32),
                        pltpu.VMEM((L // IDX_TK, IDX_TK, tq), BF16)],
        compiler_params=_params(("parallel", "parallel")),
        name="index_mask",
    )(qi3, ki3, wt3)


ATT_TQ = 512
ATT_TK = 1024


def _attn_kernel(q_ref, k_ref, v_ref, m_ref, o_ref, acc_ref, mx_ref):
    tq = q_ref.shape[1]
    tk = k_ref.shape[1]
    ngrp = tk // LANES
    i = pl.program_id(1)
    kb = pl.program_id(2)
    last = ((i + 1) * tq - 1) // tk

    @pl.when(kb == 0)
    def _():
        acc_ref[...] = jnp.zeros_like(acc_ref)
        mx_ref[...] = jnp.full_like(mx_ref, -jnp.inf)

    @pl.when(kb <= last)
    def _():
        bias = m_ref[0].astype(F32)
        ones = jnp.ones((tk, HEAD_DIM), BF16)
        for h in range(N_HEADS):
            hs = slice(h * HEAD_DIM, (h + 1) * HEAD_DIM)
            s = lax.dot_general(q_ref[0, :, hs], k_ref[0, :, hs], _NT,
                                preferred_element_type=F32) + bias
            m_old = mx_ref[h]
            m_new = jnp.maximum(m_old, jnp.max(s, axis=1, keepdims=True))
            m_safe = jnp.where(m_new == -jnp.inf, 0.0, m_new)
            alpha = jnp.exp2(m_old - m_safe)
            p = jnp.exp2(s - jnp.concatenate([m_safe] * ngrp, axis=1)).astype(BF16)
            v_ext = jnp.concatenate([v_ref[0, :, hs], ones], axis=1)
            acc_ref[h] = jnp.concatenate([alpha, alpha], axis=1) * acc_ref[h] + jnp.dot(
                p, v_ext, preferred_element_type=F32)
            mx_ref[h] = m_new

    @pl.when(kb == last)
    def _():
        for h in range(N_HEADS):
            hs = slice(h * HEAD_DIM, (h + 1) * HEAD_DIM)
            a = acc_ref[h]
            o_ref[0, :, hs] = (a[:, :HEAD_DIM] / a[:, HEAD_DIM:]).astype(BF16)


def _sparse_attention(qkv3, mask):
    B, L, _ = qkv3.shape
    W = N_HEADS * HEAD_DIM
    tq, tk = ATT_TQ, ATT_TK

    def kclamp(i, kb):
        return jnp.minimum(kb, ((i + 1) * tq - 1) // tk)

    return pl.pallas_call(
        _attn_kernel,
        grid=(B, L // tq, L // tk),
        in_specs=[pl.BlockSpec((1, tq, W), lambda b, i, kb: (b, i, 0)),
                  pl.BlockSpec((1, tk, W), lambda b, i, kb: (b, kclamp(i, kb), 1)),
                  pl.BlockSpec((1, tk, W), lambda b, i, kb: (b, kclamp(i, kb), 2)),
                  pl.BlockSpec((1, tq, tk), lambda b, i, kb: (b, i, kclamp(i, kb)))],
        out_specs=pl.BlockSpec((1, tq, W), lambda b, i, kb: (b, i, 0)),
        out_shape=jax.ShapeDtypeStruct((B, L, W), BF16),
        scratch_shapes=[pltpu.VMEM((N_HEADS, tq, 2 * HEAD_DIM), F32),
                        pltpu.VMEM((N_HEADS, tq, LANES), F32)],
        compiler_params=_params(("parallel", "parallel", "arbitrary")),
        name="sparse_attn",
    )(qkv3, qkv3, qkv3, mask)


def _mix_kernel(x_ref, attn_ref, yc_ref, ga_ref, wao_ref, wmo_ref, o_ref):
    ya = jnp.dot(attn_ref[...], wao_ref[...], preferred_element_type=F32)
    m = yc_ref[...].astype(F32) + ga_ref[...].astype(F32) * ya
    o_ref[...] = x_ref[...] + jnp.dot(m.astype(BF16), wmo_ref[...], preferred_element_type=F32)


def _mix(x2, attn2, yc2, gates2, w_ao, w_mo, tm):
    T, D = x2.shape
    row = lambda i: (i, 0)
    return pl.pallas_call(
        _mix_kernel,
        grid=(T // tm,),
        in_specs=[pl.BlockSpec((tm, D), row),
                  pl.BlockSpec((tm, D), row),
                  pl.BlockSpec((tm, D), row),
                  pl.BlockSpec((tm, D), lambda i: (i, 1)),
                  pl.BlockSpec((D, D), lambda i: (0, 0)),
                  pl.BlockSpec((D, D), lambda i: (0, 0))],
        out_specs=pl.BlockSpec((tm, D), row),
        out_shape=jax.ShapeDtypeStruct((T, D), F32),
        compiler_params=_params(("parallel",)),
        name="mix_out",
    )(x2, attn2, yc2, gates2, w_ao, w_mo)


def _memkv_kernel(mem_ref, g_ref, w_ref, o_ref):
    mn = _rms_bf16(mem_ref[0], g_ref[...])
    o_ref[0] = jnp.dot(mn, w_ref[...], preferred_element_type=F32).astype(BF16)


def _xattn_kernel(h_ref, g_ref, wq_ref, kv_ref, wo_ref, o_ref):
    h = h_ref[0]
    hn = _rms_bf16(h, g_ref[...])
    q = (jnp.dot(hn, wq_ref[...], preferred_element_type=F32) * (X_HEAD_DIM ** -0.5)).astype(BF16)
    outs = []
    for hh in range(X_HEADS):
        hs = slice(hh * X_HEAD_DIM, (hh + 1) * X_HEAD_DIM)
        vs = slice(X_W + hh * X_HEAD_DIM, X_W + (hh + 1) * X_HEAD_DIM)
        s = lax.dot_general(q[:, hs], kv_ref[0, :, hs], _NT, preferred_element_type=F32)
        p = jnp.exp(s - jnp.max(s, axis=1, keepdims=True))
        l = jnp.sum(p, axis=1, keepdims=True)
        o = jnp.dot(p.astype(BF16), kv_ref[0, :, vs], preferred_element_type=F32) / l
        outs.append(o.astype(BF16))
    o = jnp.concatenate(outs, axis=1)
    o_ref[0] = h + jnp.dot(o, wo_ref[...], preferred_element_type=F32)


def _cross_attention(h3, mem, g_x, g_mem, wq, wkv, wo, tm):
    B, L, D = h3.shape
    n_mem = mem.shape[1]
    kv = pl.pallas_call(
        _memkv_kernel,
        grid=(B,),
        in_specs=[pl.BlockSpec((1, n_mem, D), lambda b: (b, 0, 0)),
                  pl.BlockSpec((1, D), lambda b: (0, 0)),
                  pl.BlockSpec((D, 2 * X_W), lambda b: (0, 0))],
        out_specs=pl.BlockSpec((1, n_mem, 2 * X_W), lambda b: (b, 0, 0)),
        out_shape=jax.ShapeDtypeStruct((B, n_mem, 2 * X_W), BF16),
        compiler_params=_params(("parallel",)),
        name="mem_kv",
    )(mem, g_mem, wkv)

    return pl.pallas_call(
        _xattn_kernel,
        grid=(B, L // tm),
        in_specs=[pl.BlockSpec((1, tm, D), lambda b, i: (b, i, 0)),
                  pl.BlockSpec((1, D), lambda b, i: (0, 0)),
                  pl.BlockSpec((D, X_W), lambda b, i: (0, 0)),
                  pl.BlockSpec((1, n_mem, 2 * X_W), lambda b, i: (b, 0, 0)),
                  pl.BlockSpec((X_W, D), lambda b, i: (0, 0))],
        out_specs=pl.BlockSpec((1, tm, D), lambda b, i: (b, i, 0)),
        out_shape=jax.ShapeDtypeStruct((B, L, D), F32),
        compiler_params=_params(("parallel", "parallel")),
        name="cross_attn",
    )(h3, g_x, wq, kv, wo)


def _ffn_kernel(h_ref, g_ref, w1_ref, w2_ref, gf_ref, o_ref, hn_ref, acc_ref):
    j = pl.program_id(1)

    @pl.when(j == 0)
    def _():
        h = h_ref[...]
        hn_ref[...] = _rms_bf16(h, g_ref[...])
        acc_ref[...] = h

    a = jnp.dot(hn_ref[...], w1_ref[...], preferred_element_type=F32)
    r = jnp.maximum(a, 0.0)
    acc_ref[...] += jnp.dot((r * r).astype(BF16), w2_ref[...], preferred_element_type=F32)

    @pl.when(j == pl.num_programs(1) - 1)
    def _():
        y = acc_ref[...]
        ms = jnp.mean(y * y, axis=-1, keepdims=True)
        o_ref[...] = y * lax.rsqrt(ms + EPS) * gf_ref[...]


def _ffn(h2, g, w1, w2, g_final, tm, tf):
    T, D = h2.shape
    F = w1.shape[1]
    return pl.pallas_call(
        _ffn_kernel,
        grid=(T // tm, F // tf),
        in_specs=[pl.BlockSpec((tm, D), lambda i, j: (i, 0)),
                  pl.BlockSpec((1, D), lambda i, j: (0, 0)),
                  pl.BlockSpec((D, tf), lambda i, j: (0, j)),
                  pl.BlockSpec((tf, D), lambda i, j: (j, 0)),
                  pl.BlockSpec((1, D), lambda i, j: (0, 0))],
        out_specs=pl.BlockSpec((tm, D), lambda i, j: (i, 0)),
        out_shape=jax.ShapeDtypeStruct((T, D), F32),
        scratch_shapes=[pltpu.VMEM((tm, D), BF16), pltpu.VMEM((tm, D), F32)],
        compiler_params=_params(("parallel", "arbitrary")),
        name="ffn",
    )(h2, g, w1, w2, g_final)


def _layer(h3, mem, norm_mix_g, w_in, b_gate, conv_w, conv_b, conv_ln_g, conv_ln_b,
           w_conv_out, w_attn_out, w_mix_out, norm_x_g, norm_mem_g, wx_q, wx_kv, wx_o,
           norm_ffn_g, w_ff1, w_ff2, final_g):
    B, L, D = h3.shape
    T = B * L
    tm = 512
    k_top = min(TOPK_MAX, L // 4)
    row = lambda v: v.reshape(1, -1)

    n_kw = IDX_DIM + IDX_HEADS
    w_all = jnp.concatenate(
        [w_in[:, :OFF_KW], jnp.pad(w_in[:, OFF_KW:OFF_KW + n_kw], ((0, 0), (0, LANES - n_kw))),
         w_in[:, OFF_KW + n_kw:]], axis=1).astype(BF16)
    assert w_all.shape[1] == PROJ_WIDTH

    x2 = h3.reshape(T, D)
    u, qkv, qi, ki, kw, gates = _projections(x2, row(norm_mix_g), w_all, row(b_gate), tm)

    yc = _conv_branch(u.reshape(B, L, D), gates.reshape(B, L, 2 * D), conv_w, row(conv_b),
                      row(conv_ln_g), row(conv_ln_b), w_conv_out.astype(BF16), 512)

    wt = kw.reshape(B, L, LANES)[:, :, IDX_DIM:IDX_DIM + IDX_HEADS].transpose(0, 2, 1)
    mask = _index_mask(qi.reshape(B, L, -1), ki.reshape(B, L, IDX_DIM), wt, k_top)
    attn = _sparse_attention(qkv.reshape(B, L, 3 * D), mask)

    h1 = _mix(x2, attn.reshape(T, D), yc.reshape(T, D), gates,
              w_attn_out.astype(BF16), w_mix_out.astype(BF16), tm)

    h2 = _cross_attention(h1.reshape(B, L, D), mem, row(norm_x_g), row(norm_mem_g),
                          wx_q.astype(BF16), wx_kv.astype(BF16), wx_o.astype(BF16), tm)

    out = _ffn(h2.reshape(T, D), row(norm_ffn_g), w_ff1.astype(BF16), w_ff2.astype(BF16),
               final_g, 1024, 1024)
    return out.reshape(B, L, D)


def kernel(x, mem, norm_mix_g, w_in, b_gate, conv_w, conv_b, conv_ln_g, conv_ln_b, w_conv_out,
           w_attn_out, w_mix_out, norm_x_g, norm_mem_g, wx_q, wx_kv, wx_o, norm_ffn_g, w_ff1,
           w_ff2, norm_final_g):
    depth = w_in.shape[0]
    assert depth == 1, "final RMSNorm is fused into the last layer's MLP kernel"
    assert x.shape[-1] == D_MODEL
    return _layer(x, mem, norm_mix_g[0], w_in[0], b_gate[0], conv_w[0], conv_b[0], conv_ln_g[0],
                  conv_ln_b[0], w_conv_out[0], w_attn_out[0], w_mix_out[0], norm_x_g[0],
                  norm_mem_g[0], wx_q[0], wx_kv[0], wx_o[0], norm_ffn_g[0], w_ff1[0], w_ff2[0],
                  norm_final_g.reshape(1, -1))
```

```python
import functools
import math

import numpy as np
import jax
import jax.numpy as jnp
from jax import lax
from jax.experimental import pallas as pl
from jax.experimental.pallas import tpu as pltpu

F32 = jnp.float32
BF16 = jnp.bfloat16
I32 = jnp.int32

EPS = 1e-6
D_MODEL = 1024
N_HEADS = 8
HEAD_DIM = 128
IDX_HEADS = 8
IDX_DIM = 64
TOPK_MAX = 256
CONV_WIDTH = 31
X_HEADS = 4
X_HEAD_DIM = 128
X_W = X_HEADS * X_HEAD_DIM
LANES = 128
LOG2E = math.log2(math.e)

INT_MIN = -(2 ** 31)
KEY_NEG_INF = int(np.int32(np.uint32(0xFF800000) ^ np.uint32(0x7FFFFFFF)))
HI_NEG_INF = KEY_NEG_INF >> 16
HI_MIN_NORMAL = 0x0080

VMEM_LIMIT = 56 * 1024 * 1024

_NT = (((1,), (1,)), ((), ()))


def _params(sem):
    return pltpu.CompilerParams(dimension_semantics=sem, vmem_limit_bytes=VMEM_LIMIT)


def _rms_bf16(x, g):
    ms = jnp.mean(x * x, axis=-1, keepdims=True)
    return (x * lax.rsqrt(ms + EPS) * g).astype(BF16)


def _sigmoid(x):
    return 1.0 / (1.0 + jnp.exp(-x))


N_QI = IDX_HEADS * IDX_DIM
OFF_A = 0
OFF_G = OFF_A + D_MODEL
OFF_QKV = OFF_G + D_MODEL
OFF_QI = OFF_QKV + 3 * D_MODEL
OFF_KW = OFF_QI + N_QI
OFF_GATE = OFF_KW + LANES
PROJ_WIDTH = OFF_GATE + 2 * D_MODEL
PROJ_TN = 512


def _proj_kernel(x_ref, g_ref, w_ref, b_ref, u_ref, qkv_ref, qi_ref, ki_ref, kw_ref, gate_ref):
    D = D_MODEL
    xn = _rms_bf16(x_ref[...], g_ref[...])

    def mm(lo, width):
        return jnp.dot(xn, w_ref[:, lo:lo + width], preferred_element_type=F32)

    for c in range(0, D, PROJ_TN):
        u_ref[:, c:c + PROJ_TN] = (mm(OFF_A + c, PROJ_TN) *
                                   _sigmoid(mm(OFF_G + c, PROJ_TN))).astype(BF16)
    for c in range(0, 3 * D, PROJ_TN):
        scale = HEAD_DIM ** -0.5 * LOG2E if c < D else 1.0
        qkv_ref[:, c:c + PROJ_TN] = (mm(OFF_QKV + c, PROJ_TN) * scale).astype(BF16)
    qi_ref[...] = (mm(OFF_QI, N_QI) * (IDX_DIM ** -0.5)).astype(BF16)
    kw = mm(OFF_KW, LANES)
    ki_ref[...] = kw[:, :IDX_DIM].astype(BF16)
    kw_ref[...] = kw * (IDX_HEADS ** -0.5)
    for c in range(0, 2 * D, PROJ_TN):
        gate_ref[:, c:c + PROJ_TN] = _sigmoid(
            mm(OFF_GATE + c, PROJ_TN) + b_ref[:, c:c + PROJ_TN]).astype(BF16)


def _projections(x2, g, w_all, b_gate, tm):
    T, D = x2.shape
    row = lambda i: (i, 0)
    fixed = lambda i: (0, 0)
    widths = (D, 3 * D, N_QI, IDX_DIM, LANES, 2 * D)
    dtypes = (BF16, BF16, BF16, BF16, F32, BF16)
    return pl.pallas_call(
        _proj_kernel,
        grid=(T // tm,),
        in_specs=[pl.BlockSpec((tm, D), row),
                  pl.BlockSpec((1, D), fixed),
                  pl.BlockSpec((D, PROJ_WIDTH), fixed),
                  pl.BlockSpec((1, 2 * D), fixed)],
        out_specs=[pl.BlockSpec((tm, w), row) for w in widths],
        out_shape=[jax.ShapeDtypeStruct((T, w), dt) for w, dt in zip(widths, dtypes)],
        compiler_params=_params(("parallel",)),
        name="proj_in",
    )(x2, g, w_all, b_gate)


CONV_HALO = 32
CONV_ROWS = 64
CONV_LANES = 256


def _conv_kernel(u_ref, halo_ref, cw_ref, cb_ref, lng_ref, lnb_ref, wout_ref, gc_ref,
                 o_ref, win_ref, cv_ref):
    tl = u_ref.shape[1]
    C = u_ref.shape[2]
    i = pl.program_id(1)
    halo = halo_ref[0].astype(F32)
    win_ref[0:CONV_HALO, :] = jnp.where(i > 0, halo, 0.0)
    win_ref[CONV_HALO:CONV_HALO + tl, :] = u_ref[0].astype(F32)
    win_ref[CONV_HALO + tl:, :] = jnp.zeros((8, C), F32)

    s0 = CONV_HALO - (CONV_WIDTH - 1)
    for lc in range(C // CONV_LANES):
        lanes = slice(lc * CONV_LANES, (lc + 1) * CONV_LANES)

        def body(r, carry, lanes=lanes):
            r0 = pl.multiple_of(r * CONV_ROWS, CONV_ROWS)
            out = jnp.zeros((CONV_ROWS, CONV_LANES), F32)
            for res in range(8):
                part = None
                for j in range(CONV_WIDTH):
                    s = j + s0
                    if s % 8 != res:
                        continue
                    a = s - res
                    term = cw_ref[j:j + 1, lanes] * win_ref[pl.ds(r0 + a, CONV_ROWS + 8), lanes]
                    part = term if part is None else part + term
                if part is not None:
                    out = out + part[res:res + CONV_ROWS, :]
            cv_ref[pl.ds(r0, CONV_ROWS), lanes] = out + cb_ref[:, lanes]
            return carry

        lax.fori_loop(0, tl // CONV_ROWS, body, 0)

    y = cv_ref[...]
    mu = jnp.mean(y, axis=-1, keepdims=True)
    yc = y - mu
    var = jnp.mean(yc * yc, axis=-1, keepdims=True)
    z = yc * lax.rsqrt(var + EPS) * lng_ref[...] + lnb_ref[...]
    act = (z * _sigmoid(z)).astype(BF16)
    out = jnp.dot(act, wout_ref[...], preferred_element_type=F32)
    o_ref[0] = (gc_ref[0].astype(F32) * out).astype(BF16)


def _conv_branch(u3, gates3, conv_w, conv_b, ln_g, ln_b, w_out, tl):
    B, L, C = u3.shape
    hb = tl // CONV_HALO
    return pl.pallas_call(
        _conv_kernel,
        grid=(B, L // tl),
        in_specs=[pl.BlockSpec((1, tl, C), lambda b, i: (b, i, 0)),
                  pl.BlockSpec((1, CONV_HALO, C), lambda b, i: (b, jnp.maximum(i * hb - 1, 0), 0)),
                  pl.BlockSpec((CONV_WIDTH, C), lambda b, i: (0, 0)),
                  pl.BlockSpec((1, C), lambda b, i: (0, 0)),
                  pl.BlockSpec((1, C), lambda b, i: (0, 0)),
                  pl.BlockSpec((1, C), lambda b, i: (0, 0)),
                  pl.BlockSpec((C, C), lambda b, i: (0, 0)),
                  pl.BlockSpec((1, tl, C), lambda b, i: (b, i, 0))],
        out_specs=pl.BlockSpec((1, tl, C), lambda b, i: (b, i, 0)),
        out_shape=jax.ShapeDtypeStruct((B, L, C), BF16),
        scratch_shapes=[pltpu.VMEM((tl + CONV_HALO + 8, C), F32), pltpu.VMEM((tl, C), F32)],
        compiler_params=_params(("parallel", "parallel")),
        name="conv_branch",
    )(u3, u3, conv_w, conv_b, ln_g, ln_b, w_out, gates3)


IDX_TQ = 256
IDX_TK = 512


HI_BITS = -65536


def _hi16_as_float(hi):
    pat = hi ^ ((hi >> 31) & 0x7FFF)
    return pltpu.bitcast(pat << 16, F32)


HI_ACC_ROWS = 64
KEY_ACC_ROWS = 32


def _index_kernel(qi_ref, ki_ref, w_ref, mask_ref, key_ref, hi_ref, *, k_top):
    tq = qi_ref.shape[1]
    L = ki_ref.shape[1]
    nkc = L // IDX_TK
    i = pl.program_id(1)
    q0 = i * tq
    nk = (q0 + tq + IDX_TK - 1) // IDX_TK
    kf = float(k_top)

    def score_chunk(c, diagonal):
        k0 = pl.multiple_of(c * IDX_TK, IDX_TK)
        kc = ki_ref[0, pl.ds(k0, IDX_TK), :]
        score = jnp.zeros((IDX_TK, tq), F32)
        for h in range(IDX_HEADS):
            qh = qi_ref[0, :, h * IDX_DIM:(h + 1) * IDX_DIM]
            logit = lax.dot_general(kc, qh, _NT, preferred_element_type=F32)
            score = score + w_ref[0, h:h + 1, :] * jnp.maximum(logit, 0.0)
        bits = pltpu.bitcast(score, I32)
        bits = jnp.where((bits & 0x7F800000) == 0, 0, bits)
        key = bits ^ ((bits >> 31) & 0x7FFFFFFF)
        hi_bits = bits & HI_BITS
        if diagonal:
            kpos = k0 + lax.broadcasted_iota(I32, (IDX_TK, tq), 0)
            qpos = q0 + lax.broadcasted_iota(I32, (IDX_TK, tq), 1)
            causal = kpos <= qpos
            key = jnp.where(causal, key, INT_MIN)
            hi_bits = jnp.where(causal, hi_bits, HI_BITS)
        key_ref[c] = key
        hi_ref[c] = pltpu.bitcast(hi_bits, F32).astype(BF16)

    def score_pair(p, carry):
        score_chunk(2 * p, False)
        score_chunk(2 * p + 1, False)
        return carry

    lax.fori_loop(0, (nk - 1) // 2, score_pair, 0)

    @pl.when((nk - 1) % 2 == 1)
    def _():
        score_chunk(nk - 2, False)

    score_chunk(nk - 1, True)

    def colsum(acc):
        return jnp.sum(acc.astype(F32), axis=0, keepdims=True)

    def count_hi(cand):
        cb = jnp.broadcast_to(cand, (HI_ACC_ROWS, tq))
        one = jnp.ones((HI_ACC_ROWS, tq), BF16)
        zero = jnp.zeros((HI_ACC_ROWS, tq), BF16)

        def body(c, acc):
            for j in range(0, IDX_TK, HI_ACC_ROWS):
                acc = acc + jnp.where(hi_ref[c, j:j + HI_ACC_ROWS, :] >= cb, one, zero)
            return acc
        return colsum(lax.fori_loop(0, nk, body, zero))

    def count_key(pred, *col_args):
        args = [jnp.broadcast_to(a, (KEY_ACC_ROWS, tq)) for a in col_args]

        def body(c, acc):
            for j in range(0, IDX_TK, KEY_ACC_ROWS):
                m = pred(key_ref[c, j:j + KEY_ACC_ROWS, :], *args)
                acc = acc + jnp.where(m, 1.0, 0.0)
            return acc
        return colsum(lax.fori_loop(0, nk, body, jnp.zeros((KEY_ACC_ROWS, tq), F32)))

    def hi_body(it, carry):
        t_hi, cnt_t = carry
        cand_u = t_hi | lax.shift_left(jnp.int32(1), 15 - it)
        cand_s = cand_u - 32768
        cand_s = jnp.where((cand_s >= 1) & (cand_s < HI_MIN_NORMAL), HI_MIN_NORMAL, cand_s)
        cand_s = jnp.where((cand_s >= -HI_MIN_NORMAL) & (cand_s <= -1), 0, cand_s)
        cand = jnp.where(cand_s < HI_NEG_INF, -jnp.inf, _hi16_as_float(cand_s)).astype(BF16)
        cnt = count_hi(cand)
        take = cnt >= kf
        return jnp.where(take, cand_u, t_hi), jnp.where(take, cnt, cnt_t)

    t_hi, cnt_t = lax.fori_loop(
        0, 16, hi_body, (jnp.zeros((1, tq), I32), jnp.zeros((1, tq), F32)))

    zero_bucket = t_hi == 0x8000

    def unsettled(cnt):
        return jnp.max(jnp.where((cnt > kf) & jnp.logical_not(zero_bucket), 1.0, 0.0)) > 0.0

    def lo_step(it, t_u, cnt_t):
        cand_u = t_u | lax.shift_left(jnp.int32(1), 15 - it)
        cnt = count_key(lambda blk, cs: blk >= cs, cand_u ^ INT_MIN)
        take = cnt >= kf
        return jnp.where(take, cand_u, t_u), jnp.where(take, cnt, cnt_t)

    def lo_cond(carry):
        it, _, _, go = carry
        return jnp.logical_and(it < 16, go)

    def lo_body(carry):
        it, t_u, cnt_t, _ = carry
        t_u, cnt_t = lo_step(it, t_u, cnt_t)
        t_u, cnt_t = lo_step(it + 1, t_u, cnt_t)
        return it + 2, t_u, cnt_t, unsettled(cnt_t)

    _, t_u, cnt_t, _ = lax.while_loop(
        lo_cond, lo_body, (jnp.int32(0), t_hi << 16, cnt_t, unsettled(cnt_t)))

    t_s = jnp.maximum(t_u ^ INT_MIN, INT_MIN + 1)
    tied = cnt_t > kf
    t_w = jnp.broadcast_to(t_s, (IDX_TK, tq))

    def emit(c, sel):
        k0 = pl.multiple_of(c * IDX_TK, IDX_TK)
        blk = jnp.where(sel, 0.0, -jnp.inf).astype(F32).T
        mask_ref[0, :, pl.ds(k0, IDX_TK)] = blk.astype(BF16)

    def write_plain():
        def body(c, carry):
            emit(c, key_ref[c] >= t_w)
            return carry
        lax.fori_loop(0, nk, body, 0)

    def write_tied():
        need = kf - count_key(lambda blk, ts: blk > ts, t_s)
        need_w = jnp.broadcast_to(need, (IDX_TK, tq))
        lower = jnp.where(lax.broadcasted_iota(I32, (IDX_TK, IDX_TK), 1) <=
                          lax.broadcasted_iota(I32, (IDX_TK, IDX_TK), 0), 1.0, 0.0).astype(BF16)

        def body(c, before):
            kc = key_ref[c]
            tie = kc == t_w
            pref = jnp.dot(lower, jnp.where(tie, 1.0, 0.0).astype(BF16),
                           preferred_element_type=F32)
            sel = (kc > t_w) | (tie & (pref + before <= need_w))
            emit(c, sel)
            return before + pref[IDX_TK - 1:IDX_TK, :]
        lax.fori_loop(0, nk, body, jnp.zeros((1, tq), F32))

    any_tied = jnp.max(jnp.where(tied, 1.0, 0.0)) > 0.0
    lax.cond(any_tied, write_tied, write_plain)

    def fill_body(c, carry):
        k0 = pl.multiple_of(c * IDX_TK, IDX_TK)
        mask_ref[0, :, pl.ds(k0, IDX_TK)] = jnp.full((tq, IDX_TK), -jnp.inf, BF16)
        return carry

    lax.fori_loop(nk, nkc, fill_body, 0)


def _index_mask(qi3, ki3, wt3, k_top):
    B, L, nqi = qi3.shape
    tq = IDX_TQ
    assert IDX_TK % tq == 0 and L % IDX_TK == 0
    assert (L // IDX_TK) * (IDX_TK // HI_ACC_ROWS) <= 256, "slot counts must stay exact in bf16"
    return pl.pallas_call(
        functools.partial(_index_kernel, k_top=k_top),
        grid=(B, L // tq),
        in_specs=[pl.BlockSpec((1, tq, nqi), lambda b, i: (b, i, 0)),
                  pl.BlockSpec((1, L, IDX_DIM), lambda b, i: (b, 0, 0)),
                  pl.BlockSpec((1, IDX_HEADS, tq), lambda b, i: (b, 0, i))],
        out_specs=pl.BlockSpec((1, tq, L), lambda b, i: (b, i, 0)),
        out_shape=jax.ShapeDtypeStruct((B, L, L), BF16),
        scratch_shapes=[pltpu.VMEM((L // IDX_TK, IDX_TK, tq), I32),
                        pltpu.VMEM((L // IDX_TK, IDX_TK, tq), BF16)],
        compiler_params=_params(("parallel", "parallel")),
        name="index_mask",
    )(qi3, ki3, wt3)


ATT_TQ = 512
ATT_TK = 1024


def _attn_kernel(q_ref, k_ref, v_ref, m_ref, o_ref, acc_ref, mx_ref):
    tq = q_ref.shape[1]
    tk = k_ref.shape[1]
    ngrp = tk // LANES
    i = pl.program_id(1)
    kb = pl.program_id(2)
    last = ((i + 1) * tq - 1) // tk

    @pl.when(kb == 0)
    def _():
        acc_ref[...] = jnp.zeros_like(acc_ref)
        mx_ref[...] = jnp.full_like(mx_ref, -jnp.inf)

    @pl.when(kb <= last)
    def _():
        bias = m_ref[0].astype(F32)
        ones = jnp.ones((tk, HEAD_DIM), BF16)
        for h in range(N_HEADS):
            hs = slice(h * HEAD_DIM, (h + 1) * HEAD_DIM)
            s = lax.dot_general(q_ref[0, :, hs], k_ref[0, :, hs], _NT,
                                preferred_element_type=F32) + bias
            m_old = mx_ref[h]
            m_new = jnp.maximum(m_old, jnp.max(s, axis=1, keepdims=True))
            m_safe = jnp.where(m_new == -jnp.inf, 0.0, m_new)
            alpha = jnp.exp2(m_old - m_safe)
            p = jnp.exp2(s - jnp.concatenate([m_safe] * ngrp, axis=1)).astype(BF16)
            v_ext = jnp.concatenate([v_ref[0, :, hs], ones], axis=1)
            acc_ref[h] = jnp.concatenate([alpha, alpha], axis=1) * acc_ref[h] + jnp.dot(
                p, v_ext, preferred_element_type=F32)
            mx_ref[h] = m_new

    @pl.when(kb == last)
    def _():
        for h in range(N_HEADS):
            hs = slice(h * HEAD_DIM, (h + 1) * HEAD_DIM)
            a = acc_ref[h]
            o_ref[0, :, hs] = (a[:, :HEAD_DIM] / a[:, HEAD_DIM:]).astype(BF16)


def _sparse_attention(qkv3, mask):
    B, L, _ = qkv3.shape
    W = N_HEADS * HEAD_DIM
    tq, tk = ATT_TQ, ATT_TK

    def kclamp(i, kb):
        return jnp.minimum(kb, ((i + 1) * tq - 1) // tk)

    return pl.pallas_call(
        _attn_kernel,
        grid=(B, L // tq, L // tk),
        in_specs=[pl.BlockSpec((1, tq, W), lambda b, i, kb: (b, i, 0)),
                  pl.BlockSpec((1, tk, W), lambda b, i, kb: (b, kclamp(i, kb), 1)),
                  pl.BlockSpec((1, tk, W), lambda b, i, kb: (b, kclamp(i, kb), 2)),
                  pl.BlockSpec((1, tq, tk), lambda b, i, kb: (b, i, kclamp(i, kb)))],
        out_specs=pl.BlockSpec((1, tq, W), lambda b, i, kb: (b, i, 0)),
        out_shape=jax.ShapeDtypeStruct((B, L, W), BF16),
        scratch_shapes=[pltpu.VMEM((N_HEADS, tq, 2 * HEAD_DIM), F32),
                        pltpu.VMEM((N_HEADS, tq, LANES), F32)],
        compiler_params=_params(("parallel", "parallel", "arbitrary")),
        name="sparse_attn",
    )(qkv3, qkv3, qkv3, mask)


def _mix_residual(x, attn, yc, ga, wao_ref, wmo_ref):
    ya = jnp.dot(attn, wao_ref[...], preferred_element_type=F32)
    m = yc.astype(F32) + ga.astype(F32) * ya
    return x + jnp.dot(m.astype(BF16), wmo_ref[...], preferred_element_type=F32)


def _memkv_kernel(mem_ref, g_ref, w_ref, o_ref):
    mn = _rms_bf16(mem_ref[0], g_ref[...])
    o_ref[0] = jnp.dot(mn, w_ref[...], preferred_element_type=F32).astype(BF16)


def _cross_attn_residual(h, g_ref, wq_ref, kv_ref, wo_ref):
    hn = _rms_bf16(h, g_ref[...])
    q = (jnp.dot(hn, wq_ref[...], preferred_element_type=F32) * (X_HEAD_DIM ** -0.5)).astype(BF16)
    outs = []
    for hh in range(X_HEADS):
        hs = slice(hh * X_HEAD_DIM, (hh + 1) * X_HEAD_DIM)
        vs = slice(X_W + hh * X_HEAD_DIM, X_W + (hh + 1) * X_HEAD_DIM)
        s = lax.dot_general(q[:, hs], kv_ref[0, :, hs], _NT, preferred_element_type=F32)
        p = jnp.exp(s - jnp.max(s, axis=1, keepdims=True))
        l = jnp.sum(p, axis=1, keepdims=True)
        o = jnp.dot(p.astype(BF16), kv_ref[0, :, vs], preferred_element_type=F32) / l
        outs.append(o.astype(BF16))
    o = jnp.concatenate(outs, axis=1)
    return h + jnp.dot(o, wo_ref[...], preferred_element_type=F32)


def _memory_kv(mem, g_mem, wkv):
    B, n_mem, D = mem.shape
    return pl.pallas_call(
        _memkv_kernel,
        grid=(B,),
        in_specs=[pl.BlockSpec((1, n_mem, D), lambda b: (b, 0, 0)),
                  pl.BlockSpec((1, D), lambda b: (0, 0)),
                  pl.BlockSpec((D, 2 * X_W), lambda b: (0, 0))],
        out_specs=pl.BlockSpec((1, n_mem, 2 * X_W), lambda b: (b, 0, 0)),
        out_shape=jax.ShapeDtypeStruct((B, n_mem, 2 * X_W), BF16),
        compiler_params=_params(("parallel",)),
        name="mem_kv",
    )(mem, g_mem, wkv)


def _tail_kernel(x_ref, attn_ref, yc_ref, ga_ref, wao_ref, wmo_ref, gx_ref, wq_ref, kv_ref,
                 wo_ref, g_ref, w1_ref, w2_ref, gf_ref, o_ref, hn_ref, acc_ref):
    j = pl.program_id(1)

    @pl.when(j == 0)
    def _():
        h = _mix_residual(x_ref[...], attn_ref[...], yc_ref[...], ga_ref[...], wao_ref, wmo_ref)
        h = _cross_attn_residual(h, gx_ref, wq_ref, kv_ref, wo_ref)
        hn_ref[...] = _rms_bf16(h, g_ref[...])
        acc_ref[...] = h

    a = jnp.dot(hn_ref[...], w1_ref[...], preferred_element_type=F32)
    r = jnp.maximum(a, 0.0)
    acc_ref[...] += jnp.dot((r * r).astype(BF16), w2_ref[...], preferred_element_type=F32)

    @pl.when(j == pl.num_programs(1) - 1)
    def _():
        y = acc_ref[...]
        ms = jnp.mean(y * y, axis=-1, keepdims=True)
        o_ref[...] = y * lax.rsqrt(ms + EPS) * gf_ref[...]


def _tail(x2, attn2, yc2, gates2, w_ao, w_mo, g_x, wq, kv, wo, g_ffn, w1, w2, g_final,
          tokens_per_batch, tm, tf):
    T, D = x2.shape
    F = w1.shape[1]
    n_mem = kv.shape[1]
    tiles_per_batch = tokens_per_batch // tm
    row = lambda i, j: (i, 0)
    fixed = lambda i, j: (0, 0)
    return pl.pallas_call(
        _tail_kernel,
        grid=(T // tm, F // tf),
        in_specs=[pl.BlockSpec((tm, D), row),
                  pl.BlockSpec((tm, D), row),
                  pl.BlockSpec((tm, D), row),
                  pl.BlockSpec((tm, D), lambda i, j: (i, 1)),
                  pl.BlockSpec((D, D), fixed),
                  pl.BlockSpec((D, D), fixed),
                  pl.BlockSpec((1, D), fixed),
                  pl.BlockSpec((D, X_W), fixed),
                  pl.BlockSpec((1, n_mem, 2 * X_W), lambda i, j: (i // tiles_per_batch, 0, 0)),
                  pl.BlockSpec((X_W, D), fixed),
                  pl.BlockSpec((1, D), fixed),
                  pl.BlockSpec((D, tf), lambda i, j: (0, j)),
                  pl.BlockSpec((tf, D), lambda i, j: (j, 0)),
                  pl.BlockSpec((1, D), fixed)],
        out_specs=pl.BlockSpec((tm, D), row),
        out_shape=jax.ShapeDtypeStruct((T, D), F32),
        scratch_shapes=[pltpu.VMEM((tm, D), BF16), pltpu.VMEM((tm, D), F32)],
        compiler_params=_params(("parallel", "arbitrary")),
        name="tail",
    )(x2, attn2, yc2, gates2, w_ao, w_mo, g_x, wq, kv, wo, g_ffn, w1, w2, g_final)


def _layer(h3, mem, norm_mix_g, w_in, b_gate, conv_w, conv_b, conv_ln_g, conv_ln_b,
           w_conv_out, w_attn_out, w_mix_out, norm_x_g, norm_mem_g, wx_q, wx_kv, wx_o,
           norm_ffn_g, w_ff1, w_ff2, final_g):
    B, L, D = h3.shape
    T = B * L
    tm = 512
    k_top = min(TOPK_MAX, L // 4)
    row = lambda v: v.reshape(1, -1)

    n_kw = IDX_DIM + IDX_HEADS
    w_all = jnp.concatenate(
        [w_in[:, :OFF_KW], jnp.pad(w_in[:, OFF_KW:OFF_KW + n_kw], ((0, 0), (0, LANES - n_kw))),
         w_in[:, OFF_KW + n_kw:]], axis=1).astype(BF16)
    assert w_all.shape[1] == PROJ_WIDTH

    x2 = h3.reshape(T, D)
    u, qkv, qi, ki, kw, gates = _projections(x2, row(norm_mix_g), w_all, row(b_gate), tm)

    yc = _conv_branch(u.reshape(B, L, D), gates.reshape(B, L, 2 * D), conv_w, row(conv_b),
                      row(conv_ln_g), row(conv_ln_b), w_conv_out.astype(BF16), 512)

    wt = kw.reshape(B, L, LANES)[:, :, IDX_DIM:IDX_DIM + IDX_HEADS].transpose(0, 2, 1)
    mask = _index_mask(qi.reshape(B, L, -1), ki.reshape(B, L, IDX_DIM), wt, k_top)
    attn = _sparse_attention(qkv.reshape(B, L, 3 * D), mask)

    kv = _memory_kv(mem, row(norm_mem_g), wx_kv.astype(BF16))
    out = _tail(x2, attn.reshape(T, D), yc.reshape(T, D), gates,
                w_attn_out.astype(BF16), w_mix_out.astype(BF16),
                row(norm_x_g), wx_q.astype(BF16), kv, wx_o.astype(BF16),
                row(norm_ffn_g), w_ff1.astype(BF16), w_ff2.astype(BF16), final_g, L, tm, 1024)
    return out.reshape(B, L, D)


def kernel(x, mem, norm_mix_g, w_in, b_gate, conv_w, conv_b, conv_ln_g, conv_ln_b, w_conv_out,
           w_attn_out, w_mix_out, norm_x_g, norm_mem_g, wx_q, wx_kv, wx_o, norm_ffn_g, w_ff1,
           w_ff2, norm_final_g):
    depth = w_in.shape[0]
    assert depth == 1, "final RMSNorm is fused into the last layer's MLP kernel"
    assert x.shape[-1] == D_MODEL
    return _layer(x, mem, norm_mix_g[0], w_in[0], b_gate[0], conv_w[0], conv_b[0], conv_ln_g[0],
                  conv_ln_b[0], w_conv_out[0], w_attn_out[0], w_mix_out[0], norm_x_g[0],
                  norm_mem_g[0], wx_q[0], wx_kv[0], wx_o[0], norm_ffn_g[0], w_ff1[0], w_ff2[0],
                  norm_final_g.reshape(1, -1))
```

```python
import functools
import math

import numpy as np
import jax
import jax.numpy as jnp
from jax import lax
from jax.experimental import pallas as pl
from jax.experimental.pallas import tpu as pltpu

F32 = jnp.float32
BF16 = jnp.bfloat16
I32 = jnp.int32

EPS = 1e-6
D_MODEL = 1024
N_HEADS = 8
HEAD_DIM = 128
IDX_HEADS = 8
IDX_DIM = 64
TOPK_MAX = 256
CONV_WIDTH = 31
X_HEADS = 4
X_HEAD_DIM = 128
X_W = X_HEADS * X_HEAD_DIM
LANES = 128
LOG2E = math.log2(math.e)

INT_MIN = -(2 ** 31)
KEY_NEG_INF = int(np.int32(np.uint32(0xFF800000) ^ np.uint32(0x7FFFFFFF)))
HI_NEG_INF = KEY_NEG_INF >> 16
HI_MIN_NORMAL = 0x0080

VMEM_LIMIT = 56 * 1024 * 1024

_NT = (((1,), (1,)), ((), ()))


def _params(sem):
    return pltpu.CompilerParams(dimension_semantics=sem, vmem_limit_bytes=VMEM_LIMIT)


def _rms_bf16(x, g):
    ms = jnp.mean(x * x, axis=-1, keepdims=True)
    return (x * lax.rsqrt(ms + EPS) * g).astype(BF16)


def _sigmoid(x):
    return 1.0 / (1.0 + jnp.exp(-x))


N_QI = IDX_HEADS * IDX_DIM
OFF_A = 0
OFF_G = OFF_A + D_MODEL
OFF_QKV = OFF_G + D_MODEL
OFF_QI = OFF_QKV + 3 * D_MODEL
OFF_KW = OFF_QI + N_QI
OFF_GATE = OFF_KW + LANES
PROJ_WIDTH = OFF_GATE + 2 * D_MODEL
PROJ_TN = 512


def _proj_kernel(x_ref, g_ref, w_ref, b_ref, u_ref, qkv_ref, qi_ref, ki_ref, kw_ref, gate_ref):
    D = D_MODEL
    xn = _rms_bf16(x_ref[...], g_ref[...])

    def mm(lo, width):
        return jnp.dot(xn, w_ref[:, lo:lo + width], preferred_element_type=F32)

    for c in range(0, D, PROJ_TN):
        u_ref[:, c:c + PROJ_TN] = (mm(OFF_A + c, PROJ_TN) *
                                   _sigmoid(mm(OFF_G + c, PROJ_TN))).astype(BF16)
    for c in range(0, 3 * D, PROJ_TN):
        scale = HEAD_DIM ** -0.5 * LOG2E if c < D else 1.0
        qkv_ref[:, c:c + PROJ_TN] = (mm(OFF_QKV + c, PROJ_TN) * scale).astype(BF16)
    qi_ref[...] = (mm(OFF_QI, N_QI) * (IDX_DIM ** -0.5)).astype(BF16)
    kw = mm(OFF_KW, LANES)
    ki_ref[...] = kw[:, :IDX_DIM].astype(BF16)
    kw_ref[...] = kw * (IDX_HEADS ** -0.5)
    for c in range(0, 2 * D, PROJ_TN):
        gate_ref[:, c:c + PROJ_TN] = _sigmoid(
            mm(OFF_GATE + c, PROJ_TN) + b_ref[:, c:c + PROJ_TN]).astype(BF16)


def _projections(x2, g, w_all, b_gate, tm):
    T, D = x2.shape
    row = lambda i: (i, 0)
    fixed = lambda i: (0, 0)
    widths = (D, 3 * D, N_QI, IDX_DIM, LANES, 2 * D)
    dtypes = (BF16, BF16, BF16, BF16, F32, BF16)
    return pl.pallas_call(
        _proj_kernel,
        grid=(T // tm,),
        in_specs=[pl.BlockSpec((tm, D), row),
                  pl.BlockSpec((1, D), fixed),
                  pl.BlockSpec((D, PROJ_WIDTH), fixed),
                  pl.BlockSpec((1, 2 * D), fixed)],
        out_specs=[pl.BlockSpec((tm, w), row) for w in widths],
        out_shape=[jax.ShapeDtypeStruct((T, w), dt) for w, dt in zip(widths, dtypes)],
        compiler_params=_params(("parallel",)),
        name="proj_in",
    )(x2, g, w_all, b_gate)


CONV_HALO = 32
CONV_ROWS = 128
CONV_LANES = 128


def _conv_kernel(u_ref, halo_ref, cw_ref, cb_ref, lng_ref, lnb_ref, wout_ref, gc_ref,
                 o_ref, win_ref, cv_ref):
    tl = u_ref.shape[1]
    C = u_ref.shape[2]
    i = pl.program_id(1)
    halo = halo_ref[0].astype(F32)
    win_ref[0:CONV_HALO, :] = jnp.where(i > 0, halo, 0.0)
    win_ref[CONV_HALO:CONV_HALO + tl, :] = u_ref[0].astype(F32)
    win_ref[CONV_HALO + tl:, :] = jnp.zeros((8, C), F32)

    s0 = CONV_HALO - (CONV_WIDTH - 1)
    for lc in range(C // CONV_LANES):
        lanes = slice(lc * CONV_LANES, (lc + 1) * CONV_LANES)

        def body(r, carry, lanes=lanes):
            r0 = pl.multiple_of(r * CONV_ROWS, CONV_ROWS)
            out = jnp.zeros((CONV_ROWS, CONV_LANES), F32)
            for res in range(8):
                part = None
                for j in range(CONV_WIDTH):
                    s = j + s0
                    if s % 8 != res:
                        continue
                    a = s - res
                    term = cw_ref[j:j + 1, lanes] * win_ref[pl.ds(r0 + a, CONV_ROWS + 8), lanes]
                    part = term if part is None else part + term
                if part is not None:
                    out = out + part[res:res + CONV_ROWS, :]
            cv_ref[pl.ds(r0, CONV_ROWS), lanes] = out + cb_ref[:, lanes]
            return carry

        lax.fori_loop(0, tl // CONV_ROWS, body, 0)

    y = cv_ref[...]
    mu = jnp.mean(y, axis=-1, keepdims=True)
    yc = y - mu
    var = jnp.mean(yc * yc, axis=-1, keepdims=True)
    z = yc * lax.rsqrt(var + EPS) * lng_ref[...] + lnb_ref[...]
    act = (z * _sigmoid(z)).astype(BF16)
    out = jnp.dot(act, wout_ref[...], preferred_element_type=F32)
    o_ref[0] = (gc_ref[0].astype(F32) * out).astype(BF16)


def _conv_branch(u3, gates3, conv_w, conv_b, ln_g, ln_b, w_out, tl):
    B, L, C = u3.shape
    hb = tl // CONV_HALO
    return pl.pallas_call(
        _conv_kernel,
        grid=(B, L // tl),
        in_specs=[pl.BlockSpec((1, tl, C), lambda b, i: (b, i, 0)),
                  pl.BlockSpec((1, CONV_HALO, C), lambda b, i: (b, jnp.maximum(i * hb - 1, 0), 0)),
                  pl.BlockSpec((CONV_WIDTH, C), lambda b, i: (0, 0)),
                  pl.BlockSpec((1, C), lambda b, i: (0, 0)),
                  pl.BlockSpec((1, C), lambda b, i: (0, 0)),
                  pl.BlockSpec((1, C), lambda b, i: (0, 0)),
                  pl.BlockSpec((C, C), lambda b, i: (0, 0)),
                  pl.BlockSpec((1, tl, C), lambda b, i: (b, i, 0))],
        out_specs=pl.BlockSpec((1, tl, C), lambda b, i: (b, i, 0)),
        out_shape=jax.ShapeDtypeStruct((B, L, C), BF16),
        scratch_shapes=[pltpu.VMEM((tl + CONV_HALO + 8, C), F32), pltpu.VMEM((tl, C), F32)],
        compiler_params=_params(("parallel", "parallel")),
        name="conv_branch",
    )(u3, u3, conv_w, conv_b, ln_g, ln_b, w_out, gates3)


IDX_TQ = 256
IDX_TK = 512


HI_BITS = -65536


def _hi16_as_float(hi):
    pat = hi ^ ((hi >> 31) & 0x7FFF)
    return pltpu.bitcast(pat << 16, F32)


HI_ACC_ROWS = 64
KEY_ACC_ROWS = 32


def _index_kernel(qi_ref, ki_ref, w_ref, mask_ref, key_ref, hi_ref, *, k_top):
    tq = qi_ref.shape[1]
    L = ki_ref.shape[1]
    nkc = L // IDX_TK
    i = pl.program_id(1)
    q0 = i * tq
    nk = (q0 + tq + IDX_TK - 1) // IDX_TK
    kf = float(k_top)

    def score_chunk(c, diagonal):
        k0 = pl.multiple_of(c * IDX_TK, IDX_TK)
        kc = ki_ref[0, pl.ds(k0, IDX_TK), :]
        score = jnp.zeros((IDX_TK, tq), F32)
        for h in range(IDX_HEADS):
            qh = qi_ref[0, :, h * IDX_DIM:(h + 1) * IDX_DIM]
            logit = lax.dot_general(kc, qh, _NT, preferred_element_type=F32)
            score = score + w_ref[0, h:h + 1, :] * jnp.maximum(logit, 0.0)
        bits = pltpu.bitcast(score, I32)
        bits = jnp.where((bits & 0x7F800000) == 0, 0, bits)
        key = bits ^ ((bits >> 31) & 0x7FFFFFFF)
        hi_bits = bits & HI_BITS
        if diagonal:
            kpos = k0 + lax.broadcasted_iota(I32, (IDX_TK, tq), 0)
            qpos = q0 + lax.broadcasted_iota(I32, (IDX_TK, tq), 1)
            causal = kpos <= qpos
            key = jnp.where(causal, key, INT_MIN)
            hi_bits = jnp.where(causal, hi_bits, HI_BITS)
        key_ref[c] = key
        hi_ref[c] = pltpu.bitcast(hi_bits, F32).astype(BF16)

    def score_pair(p, carry):
        score_chunk(2 * p, False)
        score_chunk(2 * p + 1, False)
        return carry

    lax.fori_loop(0, (nk - 1) // 2, score_pair, 0)

    @pl.when((nk - 1) % 2 == 1)
    def _():
        score_chunk(nk - 2, False)

    score_chunk(nk - 1, True)

    def colsum(acc):
        return jnp.sum(acc.astype(F32), axis=0, keepdims=True)

    def count_hi(cand):
        cb = jnp.broadcast_to(cand, (HI_ACC_ROWS, tq))
        one = jnp.ones((HI_ACC_ROWS, tq), BF16)
        zero = jnp.zeros((HI_ACC_ROWS, tq), BF16)

        def body(c, acc):
            for j in range(0, IDX_TK, HI_ACC_ROWS):
                acc = acc + jnp.where(hi_ref[c, j:j + HI_ACC_ROWS, :] >= cb, one, zero)
            return acc
        return colsum(lax.fori_loop(0, nk, body, zero))

    def count_key(pred, *col_args):
        args = [jnp.broadcast_to(a, (KEY_ACC_ROWS, tq)) for a in col_args]

        def body(c, acc):
            for j in range(0, IDX_TK, KEY_ACC_ROWS):
                m = pred(key_ref[c, j:j + KEY_ACC_ROWS, :], *args)
                acc = acc + jnp.where(m, 1.0, 0.0)
            return acc
        return colsum(lax.fori_loop(0, nk, body, jnp.zeros((KEY_ACC_ROWS, tq), F32)))

    def hi_body(it, carry):
        t_hi, cnt_t = carry
        cand_u = t_hi | lax.shift_left(jnp.int32(1), 15 - it)
        cand_s = cand_u - 32768
        cand_s = jnp.where((cand_s >= 1) & (cand_s < HI_MIN_NORMAL), HI_MIN_NORMAL, cand_s)
        cand_s = jnp.where((cand_s >= -HI_MIN_NORMAL) & (cand_s <= -1), 0, cand_s)
        cand = jnp.where(cand_s < HI_NEG_INF, -jnp.inf, _hi16_as_float(cand_s)).astype(BF16)
        cnt = count_hi(cand)
        take = cnt >= kf
        return jnp.where(take, cand_u, t_hi), jnp.where(take, cnt, cnt_t)

    t_hi, cnt_t = lax.fori_loop(
        0, 16, hi_body, (jnp.zeros((1, tq), I32), jnp.zeros((1, tq), F32)))

    zero_bucket = t_hi == 0x8000

    def unsettled(cnt):
        return jnp.max(jnp.where((cnt > kf) & jnp.logical_not(zero_bucket), 1.0, 0.0)) > 0.0

    def lo_step(it, t_u, cnt_t):
        cand_u = t_u | lax.shift_left(jnp.int32(1), 15 - it)
        cnt = count_key(lambda blk, cs: blk >= cs, cand_u ^ INT_MIN)
        take = cnt >= kf
        return jnp.where(take, cand_u, t_u), jnp.where(take, cnt, cnt_t)

    def lo_cond(carry):
        it, _, _, go = carry
        return jnp.logical_and(it < 16, go)

    def lo_body(carry):
        it, t_u, cnt_t, _ = carry
        t_u, cnt_t = lo_step(it, t_u, cnt_t)
        t_u, cnt_t = lo_step(it + 1, t_u, cnt_t)
        return it + 2, t_u, cnt_t, unsettled(cnt_t)

    _, t_u, cnt_t, _ = lax.while_loop(
        lo_cond, lo_body, (jnp.int32(0), t_hi << 16, cnt_t, unsettled(cnt_t)))

    t_s = jnp.maximum(t_u ^ INT_MIN, INT_MIN + 1)
    tied = cnt_t > kf
    t_w = jnp.broadcast_to(t_s, (IDX_TK, tq))

    def emit(c, sel):
        k0 = pl.multiple_of(c * IDX_TK, IDX_TK)
        blk = jnp.where(sel, 0.0, -jnp.inf).astype(F32).T
        mask_ref[0, :, pl.ds(k0, IDX_TK)] = blk.astype(BF16)

    def write_plain():
        def body(c, carry):
            emit(c, key_ref[c] >= t_w)
            return carry
        lax.fori_loop(0, nk, body, 0)

    def write_tied():
        need = kf - count_key(lambda blk, ts: blk > ts, t_s)
        need_w = jnp.broadcast_to(need, (IDX_TK, tq))
        lower = jnp.where(lax.broadcasted_iota(I32, (IDX_TK, IDX_TK), 1) <=
                          lax.broadcasted_iota(I32, (IDX_TK, IDX_TK), 0), 1.0, 0.0).astype(BF16)

        def body(c, before):
            kc = key_ref[c]
            tie = kc == t_w
            pref = jnp.dot(lower, jnp.where(tie, 1.0, 0.0).astype(BF16),
                           preferred_element_type=F32)
            sel = (kc > t_w) | (tie & (pref + before <= need_w))
            emit(c, sel)
            return before + pref[IDX_TK - 1:IDX_TK, :]
        lax.fori_loop(0, nk, body, jnp.zeros((1, tq), F32))

    any_tied = jnp.max(jnp.where(tied, 1.0, 0.0)) > 0.0
    lax.cond(any_tied, write_tied, write_plain)

    def fill_body(c, carry):
        k0 = pl.multiple_of(c * IDX_TK, IDX_TK)
        mask_ref[0, :, pl.ds(k0, IDX_TK)] = jnp.full((tq, IDX_TK), -jnp.inf, BF16)
        return carry

    lax.fori_loop(nk, nkc, fill_body, 0)


def _index_mask(qi3, ki3, wt3, k_top):
    B, L, nqi = qi3.shape
    tq = IDX_TQ
    assert IDX_TK % tq == 0 and L % IDX_TK == 0
    assert (L // IDX_TK) * (IDX_TK // HI_ACC_ROWS) <= 256, "slot counts must stay exact in bf16"
    return pl.pallas_call(
        functools.partial(_index_kernel, k_top=k_top),
        grid=(B, L // tq),
        in_specs=[pl.BlockSpec((1, tq, nqi), lambda b, i: (b, i, 0)),
                  pl.BlockSpec((1, L, IDX_DIM), lambda b, i: (b, 0, 0)),
                  pl.BlockSpec((1, IDX_HEADS, tq), lambda b, i: (b, 0, i))],
        out_specs=pl.BlockSpec((1, tq, L), lambda b, i: (b, i, 0)),
        out_shape=jax.ShapeDtypeStruct((B, L, L), BF16),
        scratch_shapes=[pltpu.VMEM((L // IDX_TK, IDX_TK, tq), I32),
                        pltpu.VMEM((L // IDX_TK, IDX_TK, tq), BF16)],
        compiler_params=_params(("parallel", "parallel")),
        name="index_mask",
    )(qi3, ki3, wt3)


ATT_TQ = 512
ATT_TK = 1024


def _attn_kernel(q_ref, k_ref, v_ref, m_ref, o_ref, acc_ref, mx_ref, s_ref, bias_ref):
    tq = q_ref.shape[1]
    tk = k_ref.shape[1]
    ngrp = tk // LANES
    i = pl.program_id(1)
    kb = pl.program_id(2)
    last = ((i + 1) * tq - 1) // tk

    @pl.when(kb == 0)
    def _():
        acc_ref[...] = jnp.zeros_like(acc_ref)
        mx_ref[...] = jnp.full_like(mx_ref, -jnp.inf)

    def head_cols(h):
        return pl.ds(pl.multiple_of(h * HEAD_DIM, HEAD_DIM), HEAD_DIM)

    def scores(h, slot):
        hs = head_cols(h)
        s_ref[slot] = lax.dot_general(q_ref[0, :, hs], k_ref[0, :, hs], _NT,
                                      preferred_element_type=F32) + bias_ref[...]

    def accumulate(h, slot):
        s = s_ref[slot]
        m_old = mx_ref[h]
        m_new = jnp.maximum(m_old, jnp.max(s, axis=1, keepdims=True))
        m_safe = jnp.where(m_new == -jnp.inf, 0.0, m_new)
        alpha = jnp.exp2(m_old - m_safe)
        p = jnp.exp2(s - jnp.concatenate([m_safe] * ngrp, axis=1)).astype(BF16)
        ones = jnp.ones((tk, HEAD_DIM), BF16)
        v_ext = jnp.concatenate([v_ref[0, :, head_cols(h)], ones], axis=1)
        acc_ref[h] = jnp.concatenate([alpha, alpha], axis=1) * acc_ref[h] + jnp.dot(
            p, v_ext, preferred_element_type=F32)
        mx_ref[h] = m_new

    @pl.when(kb <= last)
    def _():
        bias_ref[...] = m_ref[0].astype(F32)
        scores(0, 0)

        def pair(j, carry):
            h = 2 * j
            scores(h + 1, 1)
            accumulate(h, 0)
            scores(h + 2, 0)
            accumulate(h + 1, 1)
            return carry

        lax.fori_loop(0, N_HEADS // 2 - 1, pair, 0)
        scores(N_HEADS - 1, 1)
        accumulate(N_HEADS - 2, 0)
        accumulate(N_HEADS - 1, 1)

    @pl.when(kb == last)
    def _():
        for h in range(N_HEADS):
            hs = slice(h * HEAD_DIM, (h + 1) * HEAD_DIM)
            a = acc_ref[h]
            o_ref[0, :, hs] = (a[:, :HEAD_DIM] / a[:, HEAD_DIM:]).astype(BF16)


def _sparse_attention(qkv3, mask):
    B, L, _ = qkv3.shape
    W = N_HEADS * HEAD_DIM
    tq, tk = ATT_TQ, ATT_TK

    def kclamp(i, kb):
        return jnp.minimum(kb, ((i + 1) * tq - 1) // tk)

    return pl.pallas_call(
        _attn_kernel,
        grid=(B, L // tq, L // tk),
        in_specs=[pl.BlockSpec((1, tq, W), lambda b, i, kb: (b, i, 0)),
                  pl.BlockSpec((1, tk, W), lambda b, i, kb: (b, kclamp(i, kb), 1)),
                  pl.BlockSpec((1, tk, W), lambda b, i, kb: (b, kclamp(i, kb), 2)),
                  pl.BlockSpec((1, tq, tk), lambda b, i, kb: (b, i, kclamp(i, kb)))],
        out_specs=pl.BlockSpec((1, tq, W), lambda b, i, kb: (b, i, 0)),
        out_shape=jax.ShapeDtypeStruct((B, L, W), BF16),
        scratch_shapes=[pltpu.VMEM((N_HEADS, tq, 2 * HEAD_DIM), F32),
                        pltpu.VMEM((N_HEADS, tq, LANES), F32),
                        pltpu.VMEM((2, tq, tk), F32),
                        pltpu.VMEM((tq, tk), F32)],
        compiler_params=_params(("parallel", "parallel", "arbitrary")),
        name="sparse_attn",
    )(qkv3, qkv3, qkv3, mask)


def _mix_residual(x, attn, yc, ga, wao_ref, wmo_ref):
    ya = jnp.dot(attn, wao_ref[...], preferred_element_type=F32)
    m = yc.astype(F32) + ga.astype(F32) * ya
    return x + jnp.dot(m.astype(BF16), wmo_ref[...], preferred_element_type=F32)


def _memkv_kernel(mem_ref, g_ref, w_ref, o_ref):
    mn = _rms_bf16(mem_ref[0], g_ref[...])
    o_ref[0] = jnp.dot(mn, w_ref[...], preferred_element_type=F32).astype(BF16)


def _cross_attn_residual(h, g_ref, wq_ref, kv_ref, wo_ref):
    hn = _rms_bf16(h, g_ref[...])
    q = (jnp.dot(hn, wq_ref[...], preferred_element_type=F32) * (X_HEAD_DIM ** -0.5)).astype(BF16)
    outs = []
    for hh in range(X_HEADS):
        hs = slice(hh * X_HEAD_DIM, (hh + 1) * X_HEAD_DIM)
        vs = slice(X_W + hh * X_HEAD_DIM, X_W + (hh + 1) * X_HEAD_DIM)
        s = lax.dot_general(q[:, hs], kv_ref[0, :, hs], _NT, preferred_element_type=F32)
        p = jnp.exp(s - jnp.max(s, axis=1, keepdims=True))
        l = jnp.sum(p, axis=1, keepdims=True)
        o = jnp.dot(p.astype(BF16), kv_ref[0, :, vs], preferred_element_type=F32) / l
        outs.append(o.astype(BF16))
    o = jnp.concatenate(outs, axis=1)
    return h + jnp.dot(o, wo_ref[...], preferred_element_type=F32)


def _memory_kv(mem, g_mem, wkv):
    B, n_mem, D = mem.shape
    return pl.pallas_call(
        _memkv_kernel,
        grid=(B,),
        in_specs=[pl.BlockSpec((1, n_mem, D), lambda b: (b, 0, 0)),
                  pl.BlockSpec((1, D), lambda b: (0, 0)),
                  pl.BlockSpec((D, 2 * X_W), lambda b: (0, 0))],
        out_specs=pl.BlockSpec((1, n_mem, 2 * X_W), lambda b: (b, 0, 0)),
        out_shape=jax.ShapeDtypeStruct((B, n_mem, 2 * X_W), BF16),
        compiler_params=_params(("parallel",)),
        name="mem_kv",
    )(mem, g_mem, wkv)


def _tail_kernel(x_ref, attn_ref, yc_ref, ga_ref, wao_ref, wmo_ref, gx_ref, wq_ref, kv_ref,
                 wo_ref, g_ref, w1_ref, w2_ref, gf_ref, o_ref, hn_ref, acc_ref):
    j = pl.program_id(1)

    @pl.when(j == 0)
    def _():
        h = _mix_residual(x_ref[...], attn_ref[...], yc_ref[...], ga_ref[...], wao_ref, wmo_ref)
        h = _cross_attn_residual(h, gx_ref, wq_ref, kv_ref, wo_ref)
        hn_ref[...] = _rms_bf16(h, g_ref[...])
        acc_ref[...] = h

    a = jnp.dot(hn_ref[...], w1_ref[...], preferred_element_type=F32)
    r = jnp.maximum(a, 0.0)
    acc_ref[...] += jnp.dot((r * r).astype(BF16), w2_ref[...], preferred_element_type=F32)

    @pl.when(j == pl.num_programs(1) - 1)
    def _():
        y = acc_ref[...]
        ms = jnp.mean(y * y, axis=-1, keepdims=True)
        o_ref[...] = y * lax.rsqrt(ms + EPS) * gf_ref[...]


def _tail(x2, attn2, yc2, gates2, w_ao, w_mo, g_x, wq, kv, wo, g_ffn, w1, w2, g_final,
          tokens_per_batch, tm, tf):
    T, D = x2.shape
    F = w1.shape[1]
    n_mem = kv.shape[1]
    tiles_per_batch = tokens_per_batch // tm
    row = lambda i, j: (i, 0)
    fixed = lambda i, j: (0, 0)
    return pl.pallas_call(
        _tail_kernel,
        grid=(T // tm, F // tf),
        in_specs=[pl.BlockSpec((tm, D), row),
                  pl.BlockSpec((tm, D), row),
                  pl.BlockSpec((tm, D), row),
                  pl.BlockSpec((tm, D), lambda i, j: (i, 1)),
                  pl.BlockSpec((D, D), fixed),
                  pl.BlockSpec((D, D), fixed),
                  pl.BlockSpec((1, D), fixed),
                  pl.BlockSpec((D, X_W), fixed),
                  pl.BlockSpec((1, n_mem, 2 * X_W), lambda i, j: (i // tiles_per_batch, 0, 0)),
                  pl.BlockSpec((X_W, D), fixed),
                  pl.BlockSpec((1, D), fixed),
                  pl.BlockSpec((D, tf), lambda i, j: (0, j)),
                  pl.BlockSpec((tf, D), lambda i, j: (j, 0)),
                  pl.BlockSpec((1, D), fixed)],
        out_specs=pl.BlockSpec((tm, D), row),
        out_shape=jax.ShapeDtypeStruct((T, D), F32),
        scratch_shapes=[pltpu.VMEM((tm, D), BF16), pltpu.VMEM((tm, D), F32)],
        compiler_params=_params(("parallel", "arbitrary")),
        name="tail",
    )(x2, attn2, yc2, gates2, w_ao, w_mo, g_x, wq, kv, wo, g_ffn, w1, w2, g_final)


def _layer(h3, mem, norm_mix_g, w_in, b_gate, conv_w, conv_b, conv_ln_g, conv_ln_b,
           w_conv_out, w_attn_out, w_mix_out, norm_x_g, norm_mem_g, wx_q, wx_kv, wx_o,
           norm_ffn_g, w_ff1, w_ff2, final_g):
    B, L, D = h3.shape
    T = B * L
    tm = 512
    k_top = min(TOPK_MAX, L // 4)
    row = lambda v: v.reshape(1, -1)

    n_kw = IDX_DIM + IDX_HEADS
    w_all = jnp.concatenate(
        [w_in[:, :OFF_KW], jnp.pad(w_in[:, OFF_KW:OFF_KW + n_kw], ((0, 0), (0, LANES - n_kw))),
         w_in[:, OFF_KW + n_kw:]], axis=1).astype(BF16)
    assert w_all.shape[1] == PROJ_WIDTH

    x2 = h3.reshape(T, D)
    u, qkv, qi, ki, kw, gates = _projections(x2, row(norm_mix_g), w_all, row(b_gate), tm)

    yc = _conv_branch(u.reshape(B, L, D), gates.reshape(B, L, 2 * D), conv_w, row(conv_b),
                      row(conv_ln_g), row(conv_ln_b), w_conv_out.astype(BF16), 512)

    wt = kw.reshape(B, L, LANES)[:, :, IDX_DIM:IDX_DIM + IDX_HEADS].transpose(0, 2, 1)
    mask = _index_mask(qi.reshape(B, L, -1), ki.reshape(B, L, IDX_DIM), wt, k_top)
    attn = _sparse_attention(qkv.reshape(B, L, 3 * D), mask)

    kv = _memory_kv(mem, row(norm_mem_g), wx_kv.astype(BF16))
    out = _tail(x2, attn.reshape(T, D), yc.reshape(T, D), gates,
                w_attn_out.astype(BF16), w_mix_out.astype(BF16),
                row(norm_x_g), wx_q.astype(BF16), kv, wx_o.astype(BF16),
                row(norm_ffn_g), w_ff1.astype(BF16), w_ff2.astype(BF16), final_g, L, 1024, 512)
    return out.reshape(B, L, D)


def kernel(x, mem, norm_mix_g, w_in, b_gate, conv_w, conv_b, conv_ln_g, conv_ln_b, w_conv_out,
           w_attn_out, w_mix_out, norm_x_g, norm_mem_g, wx_q, wx_kv, wx_o, norm_ffn_g, w_ff1,
           w_ff2, norm_final_g):
    depth = w_in.shape[0]
    assert depth == 1, "final RMSNorm is fused into the last layer's MLP kernel"
    assert x.shape[-1] == D_MODEL
    return _layer(x, mem, norm_mix_g[0], w_in[0], b_gate[0], conv_w[0], conv_b[0], conv_ln_g[0],
                  conv_ln_b[0], w_conv_out[0], w_attn_out[0], w_mix_out[0], norm_x_g[0],
                  norm_mem_g[0], wx_q[0], wx_kv[0], wx_o[0], norm_ffn_g[0], w_ff1[0], w_ff2[0],
                  norm_final_g.reshape(1, -1))
```

```python
import functools
import math

import numpy as np
import jax
import jax.numpy as jnp
from jax import lax
from jax.experimental import pallas as pl
from jax.experimental.pallas import tpu as pltpu

F32 = jnp.float32
BF16 = jnp.bfloat16
I32 = jnp.int32

EPS = 1e-6
D_MODEL = 1024
N_HEADS = 8
HEAD_DIM = 128
IDX_HEADS = 8
IDX_DIM = 64
TOPK_MAX = 256
CONV_WIDTH = 31
X_HEADS = 4
X_HEAD_DIM = 128
X_W = X_HEADS * X_HEAD_DIM
LANES = 128
LOG2E = math.log2(math.e)

INT_MIN = -(2 ** 31)
KEY_NEG_INF = int(np.int32(np.uint32(0xFF800000) ^ np.uint32(0x7FFFFFFF)))
HI_NEG_INF = KEY_NEG_INF >> 16
HI_MIN_NORMAL = 0x0080

VMEM_LIMIT = 56 * 1024 * 1024

_NT = (((1,), (1,)), ((), ()))


def _params(sem):
    return pltpu.CompilerParams(dimension_semantics=sem, vmem_limit_bytes=VMEM_LIMIT)


def _rms_bf16(x, g):
    ms = jnp.mean(x * x, axis=-1, keepdims=True)
    return (x * lax.rsqrt(ms + EPS) * g).astype(BF16)


def _sigmoid(x):
    return 1.0 / (1.0 + jnp.exp(-x))


N_QI = IDX_HEADS * IDX_DIM
OFF_A = 0
OFF_G = OFF_A + D_MODEL
OFF_QKV = OFF_G + D_MODEL
OFF_QI = OFF_QKV + 3 * D_MODEL
OFF_KW = OFF_QI + N_QI
OFF_GATE = OFF_KW + LANES
PROJ_WIDTH = OFF_GATE + 2 * D_MODEL
PROJ_TN = 512


def _proj_kernel(x_ref, g_ref, w_ref, b_ref, u_ref, qkv_ref, qi_ref, ki_ref, kw_ref, gate_ref):
    D = D_MODEL
    xn = _rms_bf16(x_ref[...], g_ref[...])

    def mm(lo, width):
        return jnp.dot(xn, w_ref[:, lo:lo + width], preferred_element_type=F32)

    for c in range(0, D, PROJ_TN):
        u_ref[:, c:c + PROJ_TN] = (mm(OFF_A + c, PROJ_TN) *
                                   _sigmoid(mm(OFF_G + c, PROJ_TN))).astype(BF16)
    for c in range(0, 3 * D, PROJ_TN):
        scale = HEAD_DIM ** -0.5 * LOG2E if c < D else 1.0
        qkv_ref[:, c:c + PROJ_TN] = (mm(OFF_QKV + c, PROJ_TN) * scale).astype(BF16)
    qi_ref[...] = (mm(OFF_QI, N_QI) * (IDX_DIM ** -0.5)).astype(BF16)
    kw = mm(OFF_KW, LANES)
    ki_ref[...] = kw[:, :IDX_DIM].astype(BF16)
    kw_ref[...] = kw * (IDX_HEADS ** -0.5)
    for c in range(0, 2 * D, PROJ_TN):
        gate_ref[:, c:c + PROJ_TN] = _sigmoid(
            mm(OFF_GATE + c, PROJ_TN) + b_ref[:, c:c + PROJ_TN]).astype(BF16)


def _projections(x2, g, w_all, b_gate, tm):
    T, D = x2.shape
    row = lambda i: (i, 0)
    fixed = lambda i: (0, 0)
    widths = (D, 3 * D, N_QI, IDX_DIM, LANES, 2 * D)
    dtypes = (BF16, BF16, BF16, BF16, F32, BF16)
    return pl.pallas_call(
        _proj_kernel,
        grid=(T // tm,),
        in_specs=[pl.BlockSpec((tm, D), row),
                  pl.BlockSpec((1, D), fixed),
                  pl.BlockSpec((D, PROJ_WIDTH), fixed),
                  pl.BlockSpec((1, 2 * D), fixed)],
        out_specs=[pl.BlockSpec((tm, w), row) for w in widths],
        out_shape=[jax.ShapeDtypeStruct((T, w), dt) for w, dt in zip(widths, dtypes)],
        compiler_params=_params(("parallel",)),
        name="proj_in",
    )(x2, g, w_all, b_gate)


CONV_HALO = 32
CONV_ROWS = 128
CONV_LANES = 128


def _conv_kernel(u_ref, halo_ref, cw_ref, cb_ref, lng_ref, lnb_ref, wout_ref, gc_ref,
                 o_ref, win_ref, cv_ref):
    tl = u_ref.shape[1]
    C = u_ref.shape[2]
    i = pl.program_id(1)
    halo = halo_ref[0].astype(F32)
    win_ref[0:CONV_HALO, :] = jnp.where(i > 0, halo, 0.0)
    win_ref[CONV_HALO:CONV_HALO + tl, :] = u_ref[0].astype(F32)
    win_ref[CONV_HALO + tl:, :] = jnp.zeros((8, C), F32)

    s0 = CONV_HALO - (CONV_WIDTH - 1)
    for lc in range(C // CONV_LANES):
        lanes = slice(lc * CONV_LANES, (lc + 1) * CONV_LANES)

        def body(r, carry, lanes=lanes):
            r0 = pl.multiple_of(r * CONV_ROWS, CONV_ROWS)
            out = jnp.zeros((CONV_ROWS, CONV_LANES), F32)
            for res in range(8):
                part = None
                for j in range(CONV_WIDTH):
                    s = j + s0
                    if s % 8 != res:
                        continue
                    a = s - res
                    term = cw_ref[j:j + 1, lanes] * win_ref[pl.ds(r0 + a, CONV_ROWS + 8), lanes]
                    part = term if part is None else part + term
                if part is not None:
                    out = out + part[res:res + CONV_ROWS, :]
            cv_ref[pl.ds(r0, CONV_ROWS), lanes] = out + cb_ref[:, lanes]
            return carry

        lax.fori_loop(0, tl // CONV_ROWS, body, 0)

    y = cv_ref[...]
    mu = jnp.mean(y, axis=-1, keepdims=True)
    yc = y - mu
    var = jnp.mean(yc * yc, axis=-1, keepdims=True)
    z = yc * lax.rsqrt(var + EPS) * lng_ref[...] + lnb_ref[...]
    act = (z * _sigmoid(z)).astype(BF16)
    out = jnp.dot(act, wout_ref[...], preferred_element_type=F32)
    o_ref[0] = (gc_ref[0].astype(F32) * out).astype(BF16)


def _conv_branch(u3, gates3, conv_w, conv_b, ln_g, ln_b, w_out, tl):
    B, L, C = u3.shape
    hb = tl // CONV_HALO
    return pl.pallas_call(
        _conv_kernel,
        grid=(B, L // tl),
        in_specs=[pl.BlockSpec((1, tl, C), lambda b, i: (b, i, 0)),
                  pl.BlockSpec((1, CONV_HALO, C), lambda b, i: (b, jnp.maximum(i * hb - 1, 0), 0)),
                  pl.BlockSpec((CONV_WIDTH, C), lambda b, i: (0, 0)),
                  pl.BlockSpec((1, C), lambda b, i: (0, 0)),
                  pl.BlockSpec((1, C), lambda b, i: (0, 0)),
                  pl.BlockSpec((1, C), lambda b, i: (0, 0)),
                  pl.BlockSpec((C, C), lambda b, i: (0, 0)),
                  pl.BlockSpec((1, tl, C), lambda b, i: (b, i, 0))],
        out_specs=pl.BlockSpec((1, tl, C), lambda b, i: (b, i, 0)),
        out_shape=jax.ShapeDtypeStruct((B, L, C), BF16),
        scratch_shapes=[pltpu.VMEM((tl + CONV_HALO + 8, C), F32), pltpu.VMEM((tl, C), F32)],
        compiler_params=_params(("parallel", "parallel")),
        name="conv_branch",
    )(u3, u3, conv_w, conv_b, ln_g, ln_b, w_out, gates3)


IDX_TQ = 512
IDX_TK = 512


HI_BITS = -65536


def _hi16_as_float(hi):
    pat = hi ^ ((hi >> 31) & 0x7FFF)
    return pltpu.bitcast(pat << 16, F32)


HI_ACC_ROWS = 64
KEY_ACC_ROWS = 32


def _index_kernel(qi_ref, ki_ref, w_ref, mask_ref, key_ref, hi_ref, *, k_top):
    tq = qi_ref.shape[1]
    L = ki_ref.shape[1]
    nkc = L // IDX_TK
    i = pl.program_id(1)
    q0 = i * tq
    nk = (q0 + tq + IDX_TK - 1) // IDX_TK
    kf = float(k_top)

    def score_chunk(c, diagonal):
        k0 = pl.multiple_of(c * IDX_TK, IDX_TK)
        kc = ki_ref[0, pl.ds(k0, IDX_TK), :]
        score = jnp.zeros((IDX_TK, tq), F32)
        for h in range(IDX_HEADS):
            qh = qi_ref[0, :, h * IDX_DIM:(h + 1) * IDX_DIM]
            logit = lax.dot_general(kc, qh, _NT, preferred_element_type=F32)
            score = score + w_ref[0, h:h + 1, :] * jnp.maximum(logit, 0.0)
        bits = pltpu.bitcast(score, I32)
        bits = jnp.where((bits & 0x7F800000) == 0, 0, bits)
        key = bits ^ ((bits >> 31) & 0x7FFFFFFF)
        hi_bits = bits & HI_BITS
        if diagonal:
            kpos = k0 + lax.broadcasted_iota(I32, (IDX_TK, tq), 0)
            qpos = q0 + lax.broadcasted_iota(I32, (IDX_TK, tq), 1)
            causal = kpos <= qpos
            key = jnp.where(causal, key, INT_MIN)
            hi_bits = jnp.where(causal, hi_bits, HI_BITS)
        key_ref[c] = key
        hi_ref[c] = pltpu.bitcast(hi_bits, F32).astype(BF16)

    def score_pair(p, carry):
        score_chunk(2 * p, False)
        score_chunk(2 * p + 1, False)
        return carry

    lax.fori_loop(0, (nk - 1) // 2, score_pair, 0)

    @pl.when((nk - 1) % 2 == 1)
    def _():
        score_chunk(nk - 2, False)

    score_chunk(nk - 1, True)

    def colsum(acc):
        return jnp.sum(acc.astype(F32), axis=0, keepdims=True)

    def count_hi(cand):
        cb = jnp.broadcast_to(cand, (HI_ACC_ROWS, tq))
        one = jnp.ones((HI_ACC_ROWS, tq), BF16)
        zero = jnp.zeros((HI_ACC_ROWS, tq), BF16)

        def body(c, acc):
            for j in range(0, IDX_TK, HI_ACC_ROWS):
                acc = acc + jnp.where(hi_ref[c, j:j + HI_ACC_ROWS, :] >= cb, one, zero)
            return acc
        return colsum(lax.fori_loop(0, nk, body, zero))

    def count_key(pred, *col_args):
        args = [jnp.broadcast_to(a, (KEY_ACC_ROWS, tq)) for a in col_args]

        def body(c, acc):
            for j in range(0, IDX_TK, KEY_ACC_ROWS):
                m = pred(key_ref[c, j:j + KEY_ACC_ROWS, :], *args)
                acc = acc + jnp.where(m, 1.0, 0.0)
            return acc
        return colsum(lax.fori_loop(0, nk, body, jnp.zeros((KEY_ACC_ROWS, tq), F32)))

    def hi_body(it, carry):
        t_hi, cnt_t = carry
        cand_u = t_hi | lax.shift_left(jnp.int32(1), 15 - it)
        cand_s = cand_u - 32768
        cand_s = jnp.where((cand_s >= 1) & (cand_s < HI_MIN_NORMAL), HI_MIN_NORMAL, cand_s)
        cand_s = jnp.where((cand_s >= -HI_MIN_NORMAL) & (cand_s <= -1), 0, cand_s)
        cand = jnp.where(cand_s < HI_NEG_INF, -jnp.inf, _hi16_as_float(cand_s)).astype(BF16)
        cnt = count_hi(cand)
        take = cnt >= kf
        return jnp.where(take, cand_u, t_hi), jnp.where(take, cnt, cnt_t)

    t_hi, cnt_t = lax.fori_loop(
        0, 16, hi_body, (jnp.zeros((1, tq), I32), jnp.zeros((1, tq), F32)))

    zero_bucket = t_hi == 0x8000

    def unsettled(cnt):
        return jnp.max(jnp.where((cnt > kf) & jnp.logical_not(zero_bucket), 1.0, 0.0)) > 0.0

    def lo_step(it, t_u, cnt_t):
        cand_u = t_u | lax.shift_left(jnp.int32(1), 15 - it)
        cnt = count_key(lambda blk, cs: blk >= cs, cand_u ^ INT_MIN)
        take = cnt >= kf
        return jnp.where(take, cand_u, t_u), jnp.where(take, cnt, cnt_t)

    def lo_cond(carry):
        it, _, _, go = carry
        return jnp.logical_and(it < 16, go)

    def lo_body(carry):
        it, t_u, cnt_t, _ = carry
        t_u, cnt_t = lo_step(it, t_u, cnt_t)
        t_u, cnt_t = lo_step(it + 1, t_u, cnt_t)
        return it + 2, t_u, cnt_t, unsettled(cnt_t)

    _, t_u, cnt_t, _ = lax.while_loop(
        lo_cond, lo_body, (jnp.int32(0), t_hi << 16, cnt_t, unsettled(cnt_t)))

    t_s = jnp.maximum(t_u ^ INT_MIN, INT_MIN + 1)
    tied = cnt_t > kf
    t_w = jnp.broadcast_to(t_s, (IDX_TK, tq))

    def emit(c, sel):
        k0 = pl.multiple_of(c * IDX_TK, IDX_TK)
        blk = jnp.where(sel, 0.0, -jnp.inf).astype(F32).T
        mask_ref[0, :, pl.ds(k0, IDX_TK)] = blk.astype(BF16)

    def write_plain():
        def body(c, carry):
            emit(c, key_ref[c] >= t_w)
            return carry
        lax.fori_loop(0, nk, body, 0)

    def write_tied():
        need = kf - count_key(lambda blk, ts: blk > ts, t_s)
        need_w = jnp.broadcast_to(need, (IDX_TK, tq))
        lower = jnp.where(lax.broadcasted_iota(I32, (IDX_TK, IDX_TK), 1) <=
                          lax.broadcasted_iota(I32, (IDX_TK, IDX_TK), 0), 1.0, 0.0).astype(BF16)

        def body(c, before):
            kc = key_ref[c]
            tie = kc == t_w
            pref = jnp.dot(lower, jnp.where(tie, 1.0, 0.0).astype(BF16),
                           preferred_element_type=F32)
            sel = (kc > t_w) | (tie & (pref + before <= need_w))
            emit(c, sel)
            return before + pref[IDX_TK - 1:IDX_TK, :]
        lax.fori_loop(0, nk, body, jnp.zeros((1, tq), F32))

    any_tied = jnp.max(jnp.where(tied, 1.0, 0.0)) > 0.0
    lax.cond(any_tied, write_tied, write_plain)

    def fill_body(c, carry):
        k0 = pl.multiple_of(c * IDX_TK, IDX_TK)
        mask_ref[0, :, pl.ds(k0, IDX_TK)] = jnp.full((tq, IDX_TK), -jnp.inf, BF16)
        return carry

    lax.fori_loop(nk, nkc, fill_body, 0)


def _index_mask(qi3, ki3, wt3, k_top):
    B, L, nqi = qi3.shape
    tq = IDX_TQ
    assert IDX_TK % tq == 0 and L % IDX_TK == 0
    assert (L // IDX_TK) * (IDX_TK // HI_ACC_ROWS) <= 256, "slot counts must stay exact in bf16"
    return pl.pallas_call(
        functools.partial(_index_kernel, k_top=k_top),
        grid=(B, L // tq),
        in_specs=[pl.BlockSpec((1, tq, nqi), lambda b, i: (b, i, 0)),
                  pl.BlockSpec((1, L, IDX_DIM), lambda b, i: (b, 0, 0)),
                  pl.BlockSpec((1, IDX_HEADS, tq), lambda b, i: (b, 0, i))],
        out_specs=pl.BlockSpec((1, tq, L), lambda b, i: (b, i, 0)),
        out_shape=jax.ShapeDtypeStruct((B, L, L), BF16),
        scratch_shapes=[pltpu.VMEM((L // IDX_TK, IDX_TK, tq), I32),
                        pltpu.VMEM((L // IDX_TK, IDX_TK, tq), BF16)],
        compiler_params=_params(("parallel", "parallel")),
        name="index_mask",
    )(qi3, ki3, wt3)


ATT_TQ = 512
ATT_TK = 1024


def _attn_kernel(q_ref, k_ref, v_ref, m_ref, o_ref, acc_ref, mx_ref):
    tq = q_ref.shape[1]
    tk = k_ref.shape[1]
    ngrp = tk // LANES
    i = pl.program_id(1)
    kb = pl.program_id(2)
    last = ((i + 1) * tq - 1) // tk

    @pl.when(kb == 0)
    def _():
        acc_ref[...] = jnp.zeros_like(acc_ref)
        mx_ref[...] = jnp.full_like(mx_ref, -jnp.inf)

    @pl.when(kb <= last)
    def _():
        bias = m_ref[0].astype(F32)
        ones = jnp.ones((tk, HEAD_DIM), BF16)
        for h in range(N_HEADS):
            hs = slice(h * HEAD_DIM, (h + 1) * HEAD_DIM)
            s = lax.dot_general(q_ref[0, :, hs], k_ref[0, :, hs], _NT,
                                preferred_element_type=F32) + bias
            m_old = mx_ref[h]
            m_new = jnp.maximum(m_old, jnp.max(s, axis=1, keepdims=True))
            m_safe = jnp.where(m_new == -jnp.inf, 0.0, m_new)
            alpha = jnp.exp2(m_old - m_safe)
            p = jnp.exp2(s - jnp.concatenate([m_safe] * ngrp, axis=1)).astype(BF16)
            v_ext = jnp.concatenate([v_ref[0, :, hs], ones], axis=1)
            acc_ref[h] = jnp.concatenate([alpha, alpha], axis=1) * acc_ref[h] + jnp.dot(
                p, v_ext, preferred_element_type=F32)
            mx_ref[h] = m_new

    @pl.when(kb == last)
    def _():
        for h in range(N_HEADS):
            hs = slice(h * HEAD_DIM, (h + 1) * HEAD_DIM)
            a = acc_ref[h]
            o_ref[0, :, hs] = (a[:, :HEAD_DIM] / a[:, HEAD_DIM:]).astype(BF16)


def _sparse_attention(qkv3, mask):
    B, L, _ = qkv3.shape
    W = N_HEADS * HEAD_DIM
    tq, tk = ATT_TQ, ATT_TK

    def kclamp(i, kb):
        return jnp.minimum(kb, ((i + 1) * tq - 1) // tk)

    return pl.pallas_call(
        _attn_kernel,
        grid=(B, L // tq, L // tk),
        in_specs=[pl.BlockSpec((1, tq, W), lambda b, i, kb: (b, i, 0)),
                  pl.BlockSpec((1, tk, W), lambda b, i, kb: (b, kclamp(i, kb), 1)),
                  pl.BlockSpec((1, tk, W), lambda b, i, kb: (b, kclamp(i, kb), 2)),
                  pl.BlockSpec((1, tq, tk), lambda b, i, kb: (b, i, kclamp(i, kb)))],
        out_specs=pl.BlockSpec((1, tq, W), lambda b, i, kb: (b, i, 0)),
        out_shape=jax.ShapeDtypeStruct((B, L, W), BF16),
        scratch_shapes=[pltpu.VMEM((N_HEADS, tq, 2 * HEAD_DIM), F32),
                        pltpu.VMEM((N_HEADS, tq, LANES), F32)],
        compiler_params=_params(("parallel", "parallel", "arbitrary")),
        name="sparse_attn",
    )(qkv3, qkv3, qkv3, mask)


def _mix_residual(x, attn, yc, ga, wao_ref, wmo_ref):
    ya = jnp.dot(attn, wao_ref[...], preferred_element_type=F32)
    m = yc.astype(F32) + ga.astype(F32) * ya
    return x + jnp.dot(m.astype(BF16), wmo_ref[...], preferred_element_type=F32)


def _memkv_kernel(mem_ref, g_ref, w_ref, o_ref):
    mn = _rms_bf16(mem_ref[0], g_ref[...])
    o_ref[0] = jnp.dot(mn, w_ref[...], preferred_element_type=F32).astype(BF16)


def _cross_attn_residual(h, g_ref, wq_ref, kv_ref, wo_ref):
    hn = _rms_bf16(h, g_ref[...])
    q = (jnp.dot(hn, wq_ref[...], preferred_element_type=F32) * (X_HEAD_DIM ** -0.5)).astype(BF16)
    outs = []
    for hh in range(X_HEADS):
        hs = slice(hh * X_HEAD_DIM, (hh + 1) * X_HEAD_DIM)
        vs = slice(X_W + hh * X_HEAD_DIM, X_W + (hh + 1) * X_HEAD_DIM)
        s = lax.dot_general(q[:, hs], kv_ref[0, :, hs], _NT, preferred_element_type=F32)
        p = jnp.exp(s - jnp.max(s, axis=1, keepdims=True))
        l = jnp.sum(p, axis=1, keepdims=True)
        o = jnp.dot(p.astype(BF16), kv_ref[0, :, vs], preferred_element_type=F32) / l
        outs.append(o.astype(BF16))
    o = jnp.concatenate(outs, axis=1)
    return h + jnp.dot(o, wo_ref[...], preferred_element_type=F32)


def _memory_kv(mem, g_mem, wkv):
    B, n_mem, D = mem.shape
    return pl.pallas_call(
        _memkv_kernel,
        grid=(B,),
        in_specs=[pl.BlockSpec((1, n_mem, D), lambda b: (b, 0, 0)),
                  pl.BlockSpec((1, D), lambda b: (0, 0)),
                  pl.BlockSpec((D, 2 * X_W), lambda b: (0, 0))],
        out_specs=pl.BlockSpec((1, n_mem, 2 * X_W), lambda b: (b, 0, 0)),
        out_shape=jax.ShapeDtypeStruct((B, n_mem, 2 * X_W), BF16),
        compiler_params=_params(("parallel",)),
        name="mem_kv",
    )(mem, g_mem, wkv)


def _merge_kernel(x_ref, attn_ref, yc_ref, ga_ref, wao_ref, wmo_ref, gx_ref, wq_ref, kv_ref,
                  wo_ref, o_ref):
    h = _mix_residual(x_ref[...], attn_ref[...], yc_ref[...], ga_ref[...], wao_ref, wmo_ref)
    o_ref[...] = _cross_attn_residual(h, gx_ref, wq_ref, kv_ref, wo_ref)


def _merge(x2, attn2, yc2, gates2, w_ao, w_mo, g_x, wq, kv, wo, tokens_per_batch, tm):
    T, D = x2.shape
    n_mem = kv.shape[1]
    tiles_per_batch = tokens_per_batch // tm
    row = lambda i: (i, 0)
    fixed = lambda i: (0, 0)
    return pl.pallas_call(
        _merge_kernel,
        grid=(T // tm,),
        in_specs=[pl.BlockSpec((tm, D), row),
                  pl.BlockSpec((tm, D), row),
                  pl.BlockSpec((tm, D), row),
                  pl.BlockSpec((tm, D), lambda i: (i, 1)),
                  pl.BlockSpec((D, D), fixed),
                  pl.BlockSpec((D, D), fixed),
                  pl.BlockSpec((1, D), fixed),
                  pl.BlockSpec((D, X_W), fixed),
                  pl.BlockSpec((1, n_mem, 2 * X_W), lambda i: (i // tiles_per_batch, 0, 0)),
                  pl.BlockSpec((X_W, D), fixed)],
        out_specs=pl.BlockSpec((tm, D), row),
        out_shape=jax.ShapeDtypeStruct((T, D), F32),
        compiler_params=_params(("parallel",)),
        name="merge_xattn",
    )(x2, attn2, yc2, gates2, w_ao, w_mo, g_x, wq, kv, wo)


def _ffn_kernel(h_ref, g_ref, w1_ref, w2_ref, gf_ref, o_ref, hn_ref, acc_ref):
    j = pl.program_id(1)

    @pl.when(j == 0)
    def _():
        h = h_ref[...]
        hn_ref[...] = _rms_bf16(h, g_ref[...])
        acc_ref[...] = h

    a = jnp.dot(hn_ref[...], w1_ref[...], preferred_element_type=F32)
    r = jnp.maximum(a, 0.0)
    acc_ref[...] += jnp.dot((r * r).astype(BF16), w2_ref[...], preferred_element_type=F32)

    @pl.when(j == pl.num_programs(1) - 1)
    def _():
        y = acc_ref[...]
        ms = jnp.mean(y * y, axis=-1, keepdims=True)
        o_ref[...] = y * lax.rsqrt(ms + EPS) * gf_ref[...]


def _ffn(h2, g, w1, w2, g_final, tm, tf):
    T, D = h2.shape
    F = w1.shape[1]
    row = lambda i, j: (i, 0)
    fixed = lambda i, j: (0, 0)
    return pl.pallas_call(
        _ffn_kernel,
        grid=(T // tm, F // tf),
        in_specs=[pl.BlockSpec((tm, D), row),
                  pl.BlockSpec((1, D), fixed),
                  pl.BlockSpec((D, tf), lambda i, j: (0, j)),
                  pl.BlockSpec((tf, D), lambda i, j: (j, 0)),
                  pl.BlockSpec((1, D), fixed)],
        out_specs=pl.BlockSpec((tm, D), row),
        out_shape=jax.ShapeDtypeStruct((T, D), F32),
        scratch_shapes=[pltpu.VMEM((tm, D), BF16), pltpu.VMEM((tm, D), F32)],
        compiler_params=_params(("parallel", "arbitrary")),
        name="ffn",
    )(h2, g, w1, w2, g_final)


def _layer(h3, mem, norm_mix_g, w_in, b_gate, conv_w, conv_b, conv_ln_g, conv_ln_b,
           w_conv_out, w_attn_out, w_mix_out, norm_x_g, norm_mem_g, wx_q, wx_kv, wx_o,
           norm_ffn_g, w_ff1, w_ff2, final_g):
    B, L, D = h3.shape
    T = B * L
    tm = 512
    k_top = min(TOPK_MAX, L // 4)
    row = lambda v: v.reshape(1, -1)

    n_kw = IDX_DIM + IDX_HEADS
    w_all = jnp.concatenate(
        [w_in[:, :OFF_KW], jnp.pad(w_in[:, OFF_KW:OFF_KW + n_kw], ((0, 0), (0, LANES - n_kw))),
         w_in[:, OFF_KW + n_kw:]], axis=1).astype(BF16)
    assert w_all.shape[1] == PROJ_WIDTH

    x2 = h3.reshape(T, D)
    u, qkv, qi, ki, kw, gates = _projections(x2, row(norm_mix_g), w_all, row(b_gate), tm)

    yc = _conv_branch(u.reshape(B, L, D), gates.reshape(B, L, 2 * D), conv_w, row(conv_b),
                      row(conv_ln_g), row(conv_ln_b), w_conv_out.astype(BF16), 512)

    wt = kw.reshape(B, L, LANES)[:, :, IDX_DIM:IDX_DIM + IDX_HEADS].transpose(0, 2, 1)
    mask = _index_mask(qi.reshape(B, L, -1), ki.reshape(B, L, IDX_DIM), wt, k_top)
    attn = _sparse_attention(qkv.reshape(B, L, 3 * D), mask)

    kv = _memory_kv(mem, row(norm_mem_g), wx_kv.astype(BF16))
    h2 = _merge(x2, attn.reshape(T, D), yc.reshape(T, D), gates,
                w_attn_out.astype(BF16), w_mix_out.astype(BF16),
                row(norm_x_g), wx_q.astype(BF16), kv, wx_o.astype(BF16), L, tm)
    out = _ffn(h2, row(norm_ffn_g), w_ff1.astype(BF16), w_ff2.astype(BF16), final_g, 1024, 1024)
    return out.reshape(B, L, D)


def kernel(x, mem, norm_mix_g, w_in, b_gate, conv_w, conv_b, conv_ln_g, conv_ln_b, w_conv_out,
           w_attn_out, w_mix_out, norm_x_g, norm_mem_g, wx_q, wx_kv, wx_o, norm_ffn_g, w_ff1,
           w_ff2, norm_final_g):
    depth = w_in.shape[0]
    assert depth == 1, "final RMSNorm is fused into the last layer's MLP kernel"
    assert x.shape[-1] == D_MODEL
    return _layer(x, mem, norm_mix_g[0], w_in[0], b_gate[0], conv_w[0], conv_b[0], conv_ln_g[0],
                  conv_ln_b[0], w_conv_out[0], w_attn_out[0], w_mix_out[0], norm_x_g[0],
                  norm_mem_g[0], wx_q[0], wx_kv[0], wx_o[0], norm_ffn_g[0], w_ff1[0], w_ff2[0],
                  norm_final_g.reshape(1, -1))
```

```python
import functools
import math

import numpy as np
import jax
import jax.numpy as jnp
from jax import lax
from jax.experimental import pallas as pl
from jax.experimental.pallas import tpu as pltpu

F32 = jnp.float32
BF16 = jnp.bfloat16
I32 = jnp.int32

EPS = 1e-6
D_MODEL = 1024
N_HEADS = 8
HEAD_DIM = 128
IDX_HEADS = 8
IDX_DIM = 64
TOPK_MAX = 256
CONV_WIDTH = 31
X_HEADS = 4
X_HEAD_DIM = 128
X_W = X_HEADS * X_HEAD_DIM
LANES = 128
LOG2E = math.log2(math.e)

INT_MIN = -(2 ** 31)
KEY_NEG_INF = int(np.int32(np.uint32(0xFF800000) ^ np.uint32(0x7FFFFFFF)))
HI_NEG_INF = KEY_NEG_INF >> 16
HI_MIN_NORMAL = 0x0080

VMEM_LIMIT = 56 * 1024 * 1024

_NT = (((1,), (1,)), ((), ()))


def _params(sem):
    return pltpu.CompilerParams(dimension_semantics=sem, vmem_limit_bytes=VMEM_LIMIT)


def _rms_bf16(x, g):
    ms = jnp.mean(x * x, axis=-1, keepdims=True)
    return (x * lax.rsqrt(ms + EPS) * g).astype(BF16)


def _sigmoid(x):
    return 1.0 / (1.0 + jnp.exp(-x))


N_QI = IDX_HEADS * IDX_DIM
OFF_A = 0
OFF_G = OFF_A + D_MODEL
OFF_QKV = OFF_G + D_MODEL
OFF_QI = OFF_QKV + 3 * D_MODEL
OFF_KW = OFF_QI + N_QI
OFF_GATE = OFF_KW + LANES
PROJ_WIDTH = OFF_GATE + 2 * D_MODEL
PROJ_TN = 512


def _proj_kernel(x_ref, g_ref, w_ref, b_ref, u_ref, qkv_ref, qi_ref, ki_ref, kw_ref, gate_ref):
    D = D_MODEL
    xn = _rms_bf16(x_ref[...], g_ref[...])

    def mm(lo, width):
        return jnp.dot(xn, w_ref[:, lo:lo + width], preferred_element_type=F32)

    for c in range(0, D, PROJ_TN):
        u_ref[:, c:c + PROJ_TN] = (mm(OFF_A + c, PROJ_TN) *
                                   _sigmoid(mm(OFF_G + c, PROJ_TN))).astype(BF16)
    for c in range(0, 3 * D, PROJ_TN):
        scale = HEAD_DIM ** -0.5 * LOG2E if c < D else 1.0
        qkv_ref[:, c:c + PROJ_TN] = (mm(OFF_QKV + c, PROJ_TN) * scale).astype(BF16)
    qi_ref[...] = (mm(OFF_QI, N_QI) * (IDX_DIM ** -0.5)).astype(BF16)
    kw = mm(OFF_KW, LANES)
    ki_ref[...] = kw[:, :IDX_DIM].astype(BF16)
    kw_ref[...] = kw * (IDX_HEADS ** -0.5)
    for c in range(0, 2 * D, PROJ_TN):
        gate_ref[:, c:c + PROJ_TN] = _sigmoid(
            mm(OFF_GATE + c, PROJ_TN) + b_ref[:, c:c + PROJ_TN]).astype(BF16)


def _projections(x2, g, w_all, b_gate, tm):
    T, D = x2.shape
    row = lambda i: (i, 0)
    fixed = lambda i: (0, 0)
    widths = (D, 3 * D, N_QI, IDX_DIM, LANES, 2 * D)
    dtypes = (BF16, BF16, BF16, BF16, F32, BF16)
    return pl.pallas_call(
        _proj_kernel,
        grid=(T // tm,),
        in_specs=[pl.BlockSpec((tm, D), row),
                  pl.BlockSpec((1, D), fixed),
                  pl.BlockSpec((D, PROJ_WIDTH), fixed),
                  pl.BlockSpec((1, 2 * D), fixed)],
        out_specs=[pl.BlockSpec((tm, w), row) for w in widths],
        out_shape=[jax.ShapeDtypeStruct((T, w), dt) for w, dt in zip(widths, dtypes)],
        compiler_params=_params(("parallel",)),
        name="proj_in",
    )(x2, g, w_all, b_gate)


CONV_HALO = 32
CONV_ROWS = 128
CONV_LANES = 128


def _conv_kernel(u_ref, halo_ref, cw_ref, cb_ref, lng_ref, lnb_ref, wout_ref, gc_ref,
                 o_ref, win_ref, cv_ref):
    tl = u_ref.shape[1]
    C = u_ref.shape[2]
    i = pl.program_id(1)
    halo = halo_ref[0].astype(F32)
    win_ref[0:CONV_HALO, :] = jnp.where(i > 0, halo, 0.0)
    win_ref[CONV_HALO:CONV_HALO + tl, :] = u_ref[0].astype(F32)
    win_ref[CONV_HALO + tl:, :] = jnp.zeros((8, C), F32)

    s0 = CONV_HALO - (CONV_WIDTH - 1)
    for lc in range(C // CONV_LANES):
        lanes = slice(lc * CONV_LANES, (lc + 1) * CONV_LANES)

        def body(r, carry, lanes=lanes):
            r0 = pl.multiple_of(r * CONV_ROWS, CONV_ROWS)
            out = jnp.zeros((CONV_ROWS, CONV_LANES), F32)
            for res in range(8):
                part = None
                for j in range(CONV_WIDTH):
                    s = j + s0
                    if s % 8 != res:
                        continue
                    a = s - res
                    term = cw_ref[j:j + 1, lanes] * win_ref[pl.ds(r0 + a, CONV_ROWS + 8), lanes]
                    part = term if part is None else part + term
                if part is not None:
                    out = out + part[res:res + CONV_ROWS, :]
            cv_ref[pl.ds(r0, CONV_ROWS), lanes] = out + cb_ref[:, lanes]
            return carry

        lax.fori_loop(0, tl // CONV_ROWS, body, 0)

    y = cv_ref[...]
    mu = jnp.mean(y, axis=-1, keepdims=True)
    yc = y - mu
    var = jnp.mean(yc * yc, axis=-1, keepdims=True)
    z = yc * lax.rsqrt(var + EPS) * lng_ref[...] + lnb_ref[...]
    act = (z * _sigmoid(z)).astype(BF16)
    out = jnp.dot(act, wout_ref[...], preferred_element_type=F32)
    o_ref[0] = (gc_ref[0].astype(F32) * out).astype(BF16)


def _conv_branch(u3, gates3, conv_w, conv_b, ln_g, ln_b, w_out, tl):
    B, L, C = u3.shape
    hb = tl // CONV_HALO
    return pl.pallas_call(
        _conv_kernel,
        grid=(B, L // tl),
        in_specs=[pl.BlockSpec((1, tl, C), lambda b, i: (b, i, 0)),
                  pl.BlockSpec((1, CONV_HALO, C), lambda b, i: (b, jnp.maximum(i * hb - 1, 0), 0)),
                  pl.BlockSpec((CONV_WIDTH, C), lambda b, i: (0, 0)),
                  pl.BlockSpec((1, C), lambda b, i: (0, 0)),
                  pl.BlockSpec((1, C), lambda b, i: (0, 0)),
                  pl.BlockSpec((1, C), lambda b, i: (0, 0)),
                  pl.BlockSpec((C, C), lambda b, i: (0, 0)),
                  pl.BlockSpec((1, tl, C), lambda b, i: (b, i, 0))],
        out_specs=pl.BlockSpec((1, tl, C), lambda b, i: (b, i, 0)),
        out_shape=jax.ShapeDtypeStruct((B, L, C), BF16),
        scratch_shapes=[pltpu.VMEM((tl + CONV_HALO + 8, C), F32), pltpu.VMEM((tl, C), F32)],
        compiler_params=_params(("parallel", "parallel")),
        name="conv_branch",
    )(u3, u3, conv_w, conv_b, ln_g, ln_b, w_out, gates3)


IDX_TQ = 512
IDX_TK = 512


HI_BITS = -65536


def _hi16_as_float(hi):
    pat = hi ^ ((hi >> 31) & 0x7FFF)
    return pltpu.bitcast(pat << 16, F32)


HI_ACC_ROWS = 64
KEY_ACC_ROWS = 32
TIE_ROWS = 128


def _index_kernel(qi_ref, ki_ref, w_ref, mask_ref, key_ref, hi_ref, *, k_top):
    tq = qi_ref.shape[1]
    L = ki_ref.shape[1]
    nkc = L // IDX_TK
    i = pl.program_id(1)
    q0 = i * tq
    nk = (q0 + tq + IDX_TK - 1) // IDX_TK
    kf = float(k_top)

    def score_chunk(c, diagonal):
        k0 = pl.multiple_of(c * IDX_TK, IDX_TK)
        kc = ki_ref[0, pl.ds(k0, IDX_TK), :]
        score = jnp.zeros((IDX_TK, tq), F32)
        for h in range(IDX_HEADS):
            qh = qi_ref[0, :, h * IDX_DIM:(h + 1) * IDX_DIM]
            logit = lax.dot_general(kc, qh, _NT, preferred_element_type=F32)
            score = score + w_ref[0, h:h + 1, :] * jnp.maximum(logit, 0.0)
        bits = pltpu.bitcast(score, I32)
        bits = jnp.where((bits & 0x7F800000) == 0, 0, bits)
        key = bits ^ ((bits >> 31) & 0x7FFFFFFF)
        hi_bits = bits & HI_BITS
        if diagonal:
            kpos = k0 + lax.broadcasted_iota(I32, (IDX_TK, tq), 0)
            qpos = q0 + lax.broadcasted_iota(I32, (IDX_TK, tq), 1)
            causal = kpos <= qpos
            key = jnp.where(causal, key, INT_MIN)
            hi_bits = jnp.where(causal, hi_bits, HI_BITS)
        key_ref[c] = key
        hi_ref[c] = pltpu.bitcast(hi_bits, F32).astype(BF16)

    def score_pair(p, carry):
        score_chunk(2 * p, False)
        score_chunk(2 * p + 1, False)
        return carry

    lax.fori_loop(0, (nk - 1) // 2, score_pair, 0)

    @pl.when((nk - 1) % 2 == 1)
    def _():
        score_chunk(nk - 2, False)

    score_chunk(nk - 1, True)

    def colsum(acc):
        return jnp.sum(acc.astype(F32), axis=0, keepdims=True)

    def count_hi(cand, below=False):
        cb = jnp.broadcast_to(cand, (HI_ACC_ROWS, tq))
        one = jnp.ones((HI_ACC_ROWS, tq), BF16)
        zero = jnp.zeros((HI_ACC_ROWS, tq), BF16)

        def body(c, acc):
            for j in range(0, IDX_TK, HI_ACC_ROWS):
                blk = hi_ref[c, j:j + HI_ACC_ROWS, :]
                acc = acc + jnp.where(blk < cb if below else blk >= cb, one, zero)
            return acc
        return colsum(lax.fori_loop(0, nk, body, zero))

    def count_key(pred, *col_args):
        args = [jnp.broadcast_to(a, (KEY_ACC_ROWS, tq)) for a in col_args]

        def body(c, acc):
            for j in range(0, IDX_TK, KEY_ACC_ROWS):
                m = pred(key_ref[c, j:j + KEY_ACC_ROWS, :], *args)
                acc = acc + jnp.where(m, 1.0, 0.0)
            return acc
        return colsum(lax.fori_loop(0, nk, body, jnp.zeros((KEY_ACC_ROWS, tq), F32)))

    def hi_body(it, carry):
        t_hi, cnt_t = carry
        cand_u = t_hi | lax.shift_left(jnp.int32(1), 15 - it)
        cand_s = cand_u - 32768
        cand_s = jnp.where((cand_s >= 1) & (cand_s < HI_MIN_NORMAL), HI_MIN_NORMAL, cand_s)
        cand_s = jnp.where((cand_s >= -HI_MIN_NORMAL) & (cand_s <= -1), 0, cand_s)
        cand = jnp.where(cand_s < HI_NEG_INF, -jnp.inf, _hi16_as_float(cand_s)).astype(BF16)
        cnt = count_hi(cand)
        take = cnt >= kf
        return jnp.where(take, cand_u, t_hi), jnp.where(take, cnt, cnt_t)

    t_hi, cnt_t = lax.fori_loop(
        0, 16, hi_body, (jnp.zeros((1, tq), I32), jnp.zeros((1, tq), F32)))

    t_top = jnp.broadcast_to((t_hi << 16) ^ INT_MIN, (KEY_ACC_ROWS, tq))
    cnt_hi = cnt_t

    def byte_body(c, carry):
        for j in range(0, IDX_TK, KEY_ACC_ROWS):
            x = key_ref[c, j:j + KEY_ACC_ROWS, :] ^ t_top
            hi_ref[c, j:j + KEY_ACC_ROWS, :] = lax.shift_right_logical(x, 8).astype(F32).astype(BF16)
        return carry

    lax.fori_loop(0, nk, byte_body, 0)

    def mid_body(it, carry):
        t_mid, cnt_t = carry
        cand = t_mid | lax.shift_left(jnp.int32(1), 7 - it)
        cnt = cnt_hi - count_hi(cand.astype(F32).astype(BF16), below=True)
        take = cnt >= kf
        return jnp.where(take, cand, t_mid), jnp.where(take, cnt, cnt_t)

    t_mid, cnt_t = lax.fori_loop(0, 8, mid_body, (jnp.zeros((1, tq), I32), cnt_t))

    zero_bucket = t_hi == 0x8000

    def unsettled(cnt):
        return jnp.max(jnp.where((cnt > kf) & jnp.logical_not(zero_bucket), 1.0, 0.0)) > 0.0

    def lo_step(it, t_u, cnt_t):
        cand_u = t_u | lax.shift_left(jnp.int32(1), 7 - it)
        cnt = count_key(lambda blk, cs: blk >= cs, cand_u ^ INT_MIN)
        take = cnt >= kf
        return jnp.where(take, cand_u, t_u), jnp.where(take, cnt, cnt_t)

    def lo_cond(carry):
        it, _, _, go = carry
        return jnp.logical_and(it < 8, go)

    def lo_body(carry):
        it, t_u, cnt_t, _ = carry
        t_u, cnt_t = lo_step(it, t_u, cnt_t)
        t_u, cnt_t = lo_step(it + 1, t_u, cnt_t)
        return it + 2, t_u, cnt_t, unsettled(cnt_t)

    _, t_u, cnt_t, _ = lax.while_loop(
        lo_cond, lo_body,
        (jnp.int32(0), (t_hi << 16) | (t_mid << 8), cnt_t, unsettled(cnt_t)))

    t_s = jnp.maximum(t_u ^ INT_MIN, INT_MIN + 1)
    tied = cnt_t > kf
    t_w = jnp.broadcast_to(t_s, (IDX_TK, tq))

    def emit(c, sel):
        k0 = pl.multiple_of(c * IDX_TK, IDX_TK)
        blk = jnp.where(sel, 0.0, -jnp.inf).astype(F32).T
        mask_ref[0, :, pl.ds(k0, IDX_TK)] = blk.astype(BF16)

    def write_plain():
        def body(c, carry):
            emit(c, key_ref[c] >= t_w)
            return carry
        lax.fori_loop(0, nk, body, 0)

    def write_tied():
        need = kf - count_key(lambda blk, ts: blk > ts, t_s)
        need_w = jnp.broadcast_to(need, (IDX_TK, tq))
        lower = jnp.where(lax.broadcasted_iota(I32, (TIE_ROWS, TIE_ROWS), 1) <=
                          lax.broadcasted_iota(I32, (TIE_ROWS, TIE_ROWS), 0), 1.0, 0.0).astype(BF16)

        def body(c, before):
            kc = key_ref[c]
            tie = kc == t_w
            tie01 = jnp.where(tie, 1.0, 0.0).astype(BF16)
            ranks = []
            for j in range(0, IDX_TK, TIE_ROWS):
                r = jnp.dot(lower, tie01[j:j + TIE_ROWS, :], preferred_element_type=F32) + before
                ranks.append(r)
                before = r[TIE_ROWS - 1:TIE_ROWS, :]
            rank = jnp.concatenate(ranks, axis=0)
            emit(c, (kc > t_w) | (tie & (rank <= need_w)))
            return before
        lax.fori_loop(0, nk, body, jnp.zeros((1, tq), F32))

    any_tied = jnp.max(jnp.where(tied, 1.0, 0.0)) > 0.0
    lax.cond(any_tied, write_tied, write_plain)

    def fill_body(c, carry):
        k0 = pl.multiple_of(c * IDX_TK, IDX_TK)
        mask_ref[0, :, pl.ds(k0, IDX_TK)] = jnp.full((tq, IDX_TK), -jnp.inf, BF16)
        return carry

    lax.fori_loop(nk, nkc, fill_body, 0)


def _index_mask(qi3, ki3, wt3, k_top):
    B, L, nqi = qi3.shape
    tq = IDX_TQ
    assert IDX_TK % tq == 0 and L % IDX_TK == 0
    assert (L // IDX_TK) * (IDX_TK // HI_ACC_ROWS) <= 256, "slot counts must stay exact in bf16"
    return pl.pallas_call(
        functools.partial(_index_kernel, k_top=k_top),
        grid=(B, L // tq),
        in_specs=[pl.BlockSpec((1, tq, nqi), lambda b, i: (b, i, 0)),
                  pl.BlockSpec((1, L, IDX_DIM), lambda b, i: (b, 0, 0)),
                  pl.BlockSpec((1, IDX_HEADS, tq), lambda b, i: (b, 0, i))],
        out_specs=pl.BlockSpec((1, tq, L), lambda b, i: (b, i, 0)),
        out_shape=jax.ShapeDtypeStruct((B, L, L), BF16),
        scratch_shapes=[pltpu.VMEM((L // IDX_TK, IDX_TK, tq), I32),
                        pltpu.VMEM((L // IDX_TK, IDX_TK, tq), BF16)],
        compiler_params=_params(("parallel", "parallel")),
        name="index_mask",
    )(qi3, ki3, wt3)


ATT_TQ = 512
ATT_TK = 1024


def _attn_kernel(q_ref, k_ref, v_ref, m_ref, o_ref, acc_ref, mx_ref):
    tq = q_ref.shape[1]
    tk = k_ref.shape[1]
    ngrp = tk // LANES
    i = pl.program_id(1)
    kb = pl.program_id(2)
    last = ((i + 1) * tq - 1) // tk

    @pl.when(kb == 0)
    def _():
        acc_ref[...] = jnp.zeros_like(acc_ref)
        mx_ref[...] = jnp.full_like(mx_ref, -jnp.inf)

    @pl.when(kb <= last)
    def _():
        bias = m_ref[0].astype(F32)
        ones = jnp.ones((tk, HEAD_DIM), BF16)
        for h in range(N_HEADS):
            hs = slice(h * HEAD_DIM, (h + 1) * HEAD_DIM)
            s = lax.dot_general(q_ref[0, :, hs], k_ref[0, :, hs], _NT,
                                preferred_element_type=F32) + bias
            m_old = mx_ref[h]
            m_new = jnp.maximum(m_old, jnp.max(s, axis=1, keepdims=True))
            m_safe = jnp.where(m_new == -jnp.inf, 0.0, m_new)
            alpha = jnp.exp2(m_old - m_safe)
            p = jnp.exp2(s - jnp.concatenate([m_safe] * ngrp, axis=1)).astype(BF16)
            v_ext = jnp.concatenate([v_ref[0, :, hs], ones], axis=1)
            acc_ref[h] = jnp.concatenate([alpha, alpha], axis=1) * acc_ref[h] + jnp.dot(
                p, v_ext, preferred_element_type=F32)
            mx_ref[h] = m_new

    @pl.when(kb == last)
    def _():
        for h in range(N_HEADS):
            hs = slice(h * HEAD_DIM, (h + 1) * HEAD_DIM)
            a = acc_ref[h]
            o_ref[0, :, hs] = (a[:, :HEAD_DIM] / a[:, HEAD_DIM:]).astype(BF16)


def _sparse_attention(qkv3, mask):
    B, L, _ = qkv3.shape
    W = N_HEADS * HEAD_DIM
    tq, tk = ATT_TQ, ATT_TK

    def kclamp(i, kb):
        return jnp.minimum(kb, ((i + 1) * tq - 1) // tk)

    return pl.pallas_call(
        _attn_kernel,
        grid=(B, L // tq, L // tk),
        in_specs=[pl.BlockSpec((1, tq, W), lambda b, i, kb: (b, i, 0)),
                  pl.BlockSpec((1, tk, W), lambda b, i, kb: (b, kclamp(i, kb), 1)),
                  pl.BlockSpec((1, tk, W), lambda b, i, kb: (b, kclamp(i, kb), 2)),
                  pl.BlockSpec((1, tq, tk), lambda b, i, kb: (b, i, kclamp(i, kb)))],
        out_specs=pl.BlockSpec((1, tq, W), lambda b, i, kb: (b, i, 0)),
        out_shape=jax.ShapeDtypeStruct((B, L, W), BF16),
        scratch_shapes=[pltpu.VMEM((N_HEADS, tq, 2 * HEAD_DIM), F32),
                        pltpu.VMEM((N_HEADS, tq, LANES), F32)],
        compiler_params=_params(("parallel", "parallel", "arbitrary")),
        name="sparse_attn",
    )(qkv3, qkv3, qkv3, mask)


def _mix_residual(x, attn, yc, ga, wao_ref, wmo_ref):
    ya = jnp.dot(attn, wao_ref[...], preferred_element_type=F32)
    m = yc.astype(F32) + ga.astype(F32) * ya
    return x + jnp.dot(m.astype(BF16), wmo_ref[...], preferred_element_type=F32)


def _memkv_kernel(mem_ref, g_ref, w_ref, o_ref):
    mn = _rms_bf16(mem_ref[0], g_ref[...])
    o_ref[0] = jnp.dot(mn, w_ref[...], preferred_element_type=F32).astype(BF16)


def _cross_attn_residual(h, g_ref, wq_ref, kv_ref, wo_ref):
    hn = _rms_bf16(h, g_ref[...])
    q = (jnp.dot(hn, wq_ref[...], preferred_element_type=F32) * (X_HEAD_DIM ** -0.5)).astype(BF16)
    outs = []
    for hh in range(X_HEADS):
        hs = slice(hh * X_HEAD_DIM, (hh + 1) * X_HEAD_DIM)
        vs = slice(X_W + hh * X_HEAD_DIM, X_W + (hh + 1) * X_HEAD_DIM)
        s = lax.dot_general(q[:, hs], kv_ref[0, :, hs], _NT, preferred_element_type=F32)
        p = jnp.exp(s - jnp.max(s, axis=1, keepdims=True))
        l = jnp.sum(p, axis=1, keepdims=True)
        o = jnp.dot(p.astype(BF16), kv_ref[0, :, vs], preferred_element_type=F32) / l
        outs.append(o.astype(BF16))
    o = jnp.concatenate(outs, axis=1)
    return h + jnp.dot(o, wo_ref[...], preferred_element_type=F32)


def _memory_kv(mem, g_mem, wkv):
    B, n_mem, D = mem.shape
    return pl.pallas_call(
        _memkv_kernel,
        grid=(B,),
        in_specs=[pl.BlockSpec((1, n_mem, D), lambda b: (b, 0, 0)),
                  pl.BlockSpec((1, D), lambda b: (0, 0)),
                  pl.BlockSpec((D, 2 * X_W), lambda b: (0, 0))],
        out_specs=pl.BlockSpec((1, n_mem, 2 * X_W), lambda b: (b, 0, 0)),
        out_shape=jax.ShapeDtypeStruct((B, n_mem, 2 * X_W), BF16),
        compiler_params=_params(("parallel",)),
        name="mem_kv",
    )(mem, g_mem, wkv)


def _merge_kernel(x_ref, attn_ref, yc_ref, ga_ref, wao_ref, wmo_ref, gx_ref, wq_ref, kv_ref,
                  wo_ref, o_ref):
    h = _mix_residual(x_ref[...], attn_ref[...], yc_ref[...], ga_ref[...], wao_ref, wmo_ref)
    o_ref[...] = _cross_attn_residual(h, gx_ref, wq_ref, kv_ref, wo_ref)


def _merge(x2, attn2, yc2, gates2, w_ao, w_mo, g_x, wq, kv, wo, tokens_per_batch, tm):
    T, D = x2.shape
    n_mem = kv.shape[1]
    tiles_per_batch = tokens_per_batch // tm
    row = lambda i: (i, 0)
    fixed = lambda i: (0, 0)
    return pl.pallas_call(
        _merge_kernel,
        grid=(T // tm,),
        in_specs=[pl.BlockSpec((tm, D), row),
                  pl.BlockSpec((tm, D), row),
                  pl.BlockSpec((tm, D), row),
                  pl.BlockSpec((tm, D), lambda i: (i, 1)),
                  pl.BlockSpec((D, D), fixed),
                  pl.BlockSpec((D, D), fixed),
                  pl.BlockSpec((1, D), fixed),
                  pl.BlockSpec((D, X_W), fixed),
                  pl.BlockSpec((1, n_mem, 2 * X_W), lambda i: (i // tiles_per_batch, 0, 0)),
                  pl.BlockSpec((X_W, D), fixed)],
        out_specs=pl.BlockSpec((tm, D), row),
        out_shape=jax.ShapeDtypeStruct((T, D), F32),
        compiler_params=_params(("parallel",)),
        name="merge_xattn",
    )(x2, attn2, yc2, gates2, w_ao, w_mo, g_x, wq, kv, wo)


def _ffn_kernel(h_ref, g_ref, w1_ref, w2_ref, gf_ref, o_ref, hn_ref, acc_ref):
    j = pl.program_id(1)

    @pl.when(j == 0)
    def _():
        h = h_ref[...]
        hn_ref[...] = _rms_bf16(h, g_ref[...])
        acc_ref[...] = h

    a = jnp.dot(hn_ref[...], w1_ref[...], preferred_element_type=F32)
    r = jnp.maximum(a, 0.0)
    acc_ref[...] += jnp.dot((r * r).astype(BF16), w2_ref[...], preferred_element_type=F32)

    @pl.when(j == pl.num_programs(1) - 1)
    def _():
        y = acc_ref[...]
        ms = jnp.mean(y * y, axis=-1, keepdims=True)
        o_ref[...] = y * lax.rsqrt(ms + EPS) * gf_ref[...]


def _ffn(h2, g, w1, w2, g_final, tm, tf):
    T, D = h2.shape
    F = w1.shape[1]
    row = lambda i, j: (i, 0)
    fixed = lambda i, j: (0, 0)
    return pl.pallas_call(
        _ffn_kernel,
        grid=(T // tm, F // tf),
        in_specs=[pl.BlockSpec((tm, D), row),
                  pl.BlockSpec((1, D), fixed),
                  pl.BlockSpec((D, tf), lambda i, j: (0, j)),
                  pl.BlockSpec((tf, D), lambda i, j: (j, 0)),
                  pl.BlockSpec((1, D), fixed)],
        out_specs=pl.BlockSpec((tm, D), row),
        out_shape=jax.ShapeDtypeStruct((T, D), F32),
        scratch_shapes=[pltpu.VMEM((tm, D), BF16), pltpu.VMEM((tm, D), F32)],
        compiler_params=_params(("parallel", "arbitrary")),
        name="ffn",
    )(h2, g, w1, w2, g_final)


def _layer(h3, mem, norm_mix_g, w_in, b_gate, conv_w, conv_b, conv_ln_g, conv_ln_b,
           w_conv_out, w_attn_out, w_mix_out, norm_x_g, norm_mem_g, wx_q, wx_kv, wx_o,
           norm_ffn_g, w_ff1, w_ff2, final_g):
    B, L, D = h3.shape
    T = B * L
    tm = 512
    k_top = min(TOPK_MAX, L // 4)
    row = lambda v: v.reshape(1, -1)

    n_kw = IDX_DIM + IDX_HEADS
    w_all = jnp.concatenate(
        [w_in[:, :OFF_KW], jnp.pad(w_in[:, OFF_KW:OFF_KW + n_kw], ((0, 0), (0, LANES - n_kw))),
         w_in[:, OFF_KW + n_kw:]], axis=1).astype(BF16)
    assert w_all.shape[1] == PROJ_WIDTH

    x2 = h3.reshape(T, D)
    u, qkv, qi, ki, kw, gates = _projections(x2, row(norm_mix_g), w_all, row(b_gate), tm)

    yc = _conv_branch(u.reshape(B, L, D), gates.reshape(B, L, 2 * D), conv_w, row(conv_b),
                      row(conv_ln_g), row(conv_ln_b), w_conv_out.astype(BF16), 512)

    wt = kw.reshape(B, L, LANES)[:, :, IDX_DIM:IDX_DIM + IDX_HEADS].transpose(0, 2, 1)
    mask = _index_mask(qi.reshape(B, L, -1), ki.reshape(B, L, IDX_DIM), wt, k_top)
    attn = _sparse_attention(qkv.reshape(B, L, 3 * D), mask)

    kv = _memory_kv(mem, row(norm_mem_g), wx_kv.astype(BF16))
    h2 = _merge(x2, attn.reshape(T, D), yc.reshape(T, D), gates,
                w_attn_out.astype(BF16), w_mix_out.astype(BF16),
                row(norm_x_g), wx_q.astype(BF16), kv, wx_o.astype(BF16), L, tm)
    out = _ffn(h2, row(norm_ffn_g), w_ff1.astype(BF16), w_ff2.astype(BF16), final_g, 1024, 1024)
    return out.reshape(B, L, D)


def kernel(x, mem, norm_mix_g, w_in, b_gate, conv_w, conv_b, conv_ln_g, conv_ln_b, w_conv_out,
           w_attn_out, w_mix_out, norm_x_g, norm_mem_g, wx_q, wx_kv, wx_o, norm_ffn_g, w_ff1,
           w_ff2, norm_final_g):
    depth = w_in.shape[0]
    assert depth == 1, "final RMSNorm is fused into the last layer's MLP kernel"
    assert x.shape[-1] == D_MODEL
    return _layer(x, mem, norm_mix_g[0], w_in[0], b_gate[0], conv_w[0], conv_b[0], conv_ln_g[0],
                  conv_ln_b[0], w_conv_out[0], w_attn_out[0], w_mix_out[0], norm_x_g[0],
                  norm_mem_g[0], wx_q[0], wx_kv[0], wx_o[0], norm_ffn_g[0], w_ff1[0], w_ff2[0],
                  norm_final_g.reshape(1, -1))
```

```python
import functools
import math

import numpy as np
import jax
import jax.numpy as jnp
from jax import lax
from jax.experimental import pallas as pl
from jax.experimental.pallas import tpu as pltpu

F32 = jnp.float32
BF16 = jnp.bfloat16
I32 = jnp.int32

EPS = 1e-6
D_MODEL = 1024
N_HEADS = 8
HEAD_DIM = 128
IDX_HEADS = 8
IDX_DIM = 64
TOPK_MAX = 256
CONV_WIDTH = 31
X_HEADS = 4
X_HEAD_DIM = 128
X_W = X_HEADS * X_HEAD_DIM
LANES = 128
LOG2E = math.log2(math.e)

INT_MIN = -(2 ** 31)
F32_EXP_MASK = 0x7F800000
F32_MAG_MASK = 0x7FFFFFFF
KEY_NEG_INF = int(np.int32(np.uint32(0xFF800000) ^ np.uint32(F32_MAG_MASK)))
HI_NEG_INF = KEY_NEG_INF >> 16
HI_MIN_NORMAL = 0x0080
HI_OFFSET = 0x8000

VMEM_LIMIT = 56 * 1024 * 1024

TOKEN_TILE = 512
CONV_TILE = 512
FFN_TILE = 1024
FFN_COLS = 1024

_NT = (((1,), (1,)), ((), ()))


def _params(sem):
    return pltpu.CompilerParams(dimension_semantics=sem, vmem_limit_bytes=VMEM_LIMIT)


def _rms_bf16(x, g):
    ms = jnp.mean(x * x, axis=-1, keepdims=True)
    return (x * lax.rsqrt(ms + EPS) * g).astype(BF16)


def _sigmoid(x):
    return 1.0 / (1.0 + jnp.exp(-x))


N_QI = IDX_HEADS * IDX_DIM
OFF_A = 0
OFF_G = OFF_A + D_MODEL
OFF_QKV = OFF_G + D_MODEL
OFF_QI = OFF_QKV + 3 * D_MODEL
OFF_KW = OFF_QI + N_QI
OFF_GATE = OFF_KW + LANES
PROJ_WIDTH = OFF_GATE + 2 * D_MODEL
PROJ_TN = 512


def _proj_kernel(x_ref, g_ref, w_ref, b_ref, u_ref, qkv_ref, qi_ref, ki_ref, kw_ref, gate_ref):
    D = D_MODEL
    xn = _rms_bf16(x_ref[...], g_ref[...])

    def mm(lo, width):
        return jnp.dot(xn, w_ref[:, lo:lo + width], preferred_element_type=F32)

    for c in range(0, D, PROJ_TN):
        u_ref[:, c:c + PROJ_TN] = (mm(OFF_A + c, PROJ_TN) *
                                   _sigmoid(mm(OFF_G + c, PROJ_TN))).astype(BF16)
    for c in range(0, 3 * D, PROJ_TN):
        scale = HEAD_DIM ** -0.5 * LOG2E if c < D else 1.0
        qkv_ref[:, c:c + PROJ_TN] = (mm(OFF_QKV + c, PROJ_TN) * scale).astype(BF16)
    qi_ref[...] = (mm(OFF_QI, N_QI) * (IDX_DIM ** -0.5)).astype(BF16)
    kw = mm(OFF_KW, LANES)
    ki_ref[...] = kw[:, :IDX_DIM].astype(BF16)
    kw_ref[...] = kw * (IDX_HEADS ** -0.5)
    for c in range(0, 2 * D, PROJ_TN):
        gate_ref[:, c:c + PROJ_TN] = _sigmoid(
            mm(OFF_GATE + c, PROJ_TN) + b_ref[:, c:c + PROJ_TN]).astype(BF16)


def _projections(x2, g, w_all, b_gate, tm):
    T, D = x2.shape
    row = lambda i: (i, 0)
    fixed = lambda i: (0, 0)
    widths = (D, 3 * D, N_QI, IDX_DIM, LANES, 2 * D)
    dtypes = (BF16, BF16, BF16, BF16, F32, BF16)
    return pl.pallas_call(
        _proj_kernel,
        grid=(T // tm,),
        in_specs=[pl.BlockSpec((tm, D), row),
                  pl.BlockSpec((1, D), fixed),
                  pl.BlockSpec((D, PROJ_WIDTH), fixed),
                  pl.BlockSpec((1, 2 * D), fixed)],
        out_specs=[pl.BlockSpec((tm, w), row) for w in widths],
        out_shape=[jax.ShapeDtypeStruct((T, w), dt) for w, dt in zip(widths, dtypes)],
        compiler_params=_params(("parallel",)),
        name="proj_in",
    )(x2, g, w_all, b_gate)


CONV_HALO = 32
CONV_ROWS = 128
CONV_LANES = 128


def _conv_kernel(u_ref, halo_ref, cw_ref, cb_ref, lng_ref, lnb_ref, wout_ref, gc_ref,
                 o_ref, win_ref, cv_ref):
    tl = u_ref.shape[1]
    C = u_ref.shape[2]
    i = pl.program_id(1)
    halo = halo_ref[0].astype(F32)
    win_ref[0:CONV_HALO, :] = jnp.where(i > 0, halo, 0.0)
    win_ref[CONV_HALO:CONV_HALO + tl, :] = u_ref[0].astype(F32)
    win_ref[CONV_HALO + tl:, :] = jnp.zeros((8, C), F32)

    s0 = CONV_HALO - (CONV_WIDTH - 1)
    for lc in range(C // CONV_LANES):
        lanes = slice(lc * CONV_LANES, (lc + 1) * CONV_LANES)

        def body(r, carry, lanes=lanes):
            r0 = pl.multiple_of(r * CONV_ROWS, CONV_ROWS)
            out = jnp.zeros((CONV_ROWS, CONV_LANES), F32)
            for res in range(8):
                part = None
                for j in range(CONV_WIDTH):
                    s = j + s0
                    if s % 8 != res:
                        continue
                    a = s - res
                    term = cw_ref[j:j + 1, lanes] * win_ref[pl.ds(r0 + a, CONV_ROWS + 8), lanes]
                    part = term if part is None else part + term
                if part is not None:
                    out = out + part[res:res + CONV_ROWS, :]
            cv_ref[pl.ds(r0, CONV_ROWS), lanes] = out + cb_ref[:, lanes]
            return carry

        lax.fori_loop(0, tl // CONV_ROWS, body, 0)

    y = cv_ref[...]
    mu = jnp.mean(y, axis=-1, keepdims=True)
    yc = y - mu
    var = jnp.mean(yc * yc, axis=-1, keepdims=True)
    z = yc * lax.rsqrt(var + EPS) * lng_ref[...] + lnb_ref[...]
    act = (z * _sigmoid(z)).astype(BF16)
    out = jnp.dot(act, wout_ref[...], preferred_element_type=F32)
    o_ref[0] = (gc_ref[0].astype(F32) * out).astype(BF16)


def _conv_branch(u3, gates3, conv_w, conv_b, ln_g, ln_b, w_out, tl):
    B, L, C = u3.shape
    hb = tl // CONV_HALO
    return pl.pallas_call(
        _conv_kernel,
        grid=(B, L // tl),
        in_specs=[pl.BlockSpec((1, tl, C), lambda b, i: (b, i, 0)),
                  pl.BlockSpec((1, CONV_HALO, C), lambda b, i: (b, jnp.maximum(i * hb - 1, 0), 0)),
                  pl.BlockSpec((CONV_WIDTH, C), lambda b, i: (0, 0)),
                  pl.BlockSpec((1, C), lambda b, i: (0, 0)),
                  pl.BlockSpec((1, C), lambda b, i: (0, 0)),
                  pl.BlockSpec((1, C), lambda b, i: (0, 0)),
                  pl.BlockSpec((C, C), lambda b, i: (0, 0)),
                  pl.BlockSpec((1, tl, C), lambda b, i: (b, i, 0))],
        out_specs=pl.BlockSpec((1, tl, C), lambda b, i: (b, i, 0)),
        out_shape=jax.ShapeDtypeStruct((B, L, C), BF16),
        scratch_shapes=[pltpu.VMEM((tl + CONV_HALO + 8, C), F32), pltpu.VMEM((tl, C), F32)],
        compiler_params=_params(("parallel", "parallel")),
        name="conv_branch",
    )(u3, u3, conv_w, conv_b, ln_g, ln_b, w_out, gates3)


IDX_TQ = 512
IDX_TK = 512


HI_BITS = -65536


def _hi16_as_float(hi):
    pat = hi ^ ((hi >> 31) & (F32_MAG_MASK >> 16))
    return pltpu.bitcast(pat << 16, F32)


HI_ACC_ROWS = 64
KEY_ACC_ROWS = 32
TIE_ROWS = 128


def _index_kernel(qi_ref, ki_ref, w_ref, mask_ref, key_ref, hi_ref, *, k_top):
    tq = qi_ref.shape[1]
    L = ki_ref.shape[1]
    nkc = L // IDX_TK
    i = pl.program_id(1)
    q0 = i * tq
    nk = (q0 + tq + IDX_TK - 1) // IDX_TK
    kf = float(k_top)

    def score_chunk(c, diagonal):
        k0 = pl.multiple_of(c * IDX_TK, IDX_TK)
        kc = ki_ref[0, pl.ds(k0, IDX_TK), :]
        score = jnp.zeros((IDX_TK, tq), F32)
        for h in range(IDX_HEADS):
            qh = qi_ref[0, :, h * IDX_DIM:(h + 1) * IDX_DIM]
            logit = lax.dot_general(kc, qh, _NT, preferred_element_type=F32)
            score = score + w_ref[0, h:h + 1, :] * jnp.maximum(logit, 0.0)
        bits = pltpu.bitcast(score, I32)
        bits = jnp.where((bits & F32_EXP_MASK) == 0, 0, bits)
        key = bits ^ ((bits >> 31) & F32_MAG_MASK)
        hi_bits = bits & HI_BITS
        if diagonal:
            kpos = k0 + lax.broadcasted_iota(I32, (IDX_TK, tq), 0)
            qpos = q0 + lax.broadcasted_iota(I32, (IDX_TK, tq), 1)
            causal = kpos <= qpos
            key = jnp.where(causal, key, INT_MIN)
            hi_bits = jnp.where(causal, hi_bits, HI_BITS)
        key_ref[c] = key
        hi_ref[c] = pltpu.bitcast(hi_bits, F32).astype(BF16)

    def score_pair(p, carry):
        score_chunk(2 * p, False)
        score_chunk(2 * p + 1, False)
        return carry

    lax.fori_loop(0, (nk - 1) // 2, score_pair, 0)

    @pl.when((nk - 1) % 2 == 1)
    def _():
        score_chunk(nk - 2, False)

    score_chunk(nk - 1, True)

    def colsum(acc):
        return jnp.sum(acc.astype(F32), axis=0, keepdims=True)

    def count_hi(cand, below=False):
        cb = jnp.broadcast_to(cand, (HI_ACC_ROWS, tq))
        one = jnp.ones((HI_ACC_ROWS, tq), BF16)
        zero = jnp.zeros((HI_ACC_ROWS, tq), BF16)

        def body(c, acc):
            for j in range(0, IDX_TK, HI_ACC_ROWS):
                blk = hi_ref[c, j:j + HI_ACC_ROWS, :]
                acc = acc + jnp.where(blk < cb if below else blk >= cb, one, zero)
            return acc
        return colsum(lax.fori_loop(0, nk, body, zero))

    def count_key(pred, *col_args):
        args = [jnp.broadcast_to(a, (KEY_ACC_ROWS, tq)) for a in col_args]

        def body(c, acc):
            for j in range(0, IDX_TK, KEY_ACC_ROWS):
                m = pred(key_ref[c, j:j + KEY_ACC_ROWS, :], *args)
                acc = acc + jnp.where(m, 1.0, 0.0)
            return acc
        return colsum(lax.fori_loop(0, nk, body, jnp.zeros((KEY_ACC_ROWS, tq), F32)))

    def hi_body(it, carry):
        t_hi, cnt_t = carry
        cand_u = t_hi | lax.shift_left(jnp.int32(1), 15 - it)
        cand_s = cand_u - HI_OFFSET
        cand_s = jnp.where((cand_s >= 1) & (cand_s < HI_MIN_NORMAL), HI_MIN_NORMAL, cand_s)
        cand_s = jnp.where((cand_s >= -HI_MIN_NORMAL) & (cand_s <= -1), 0, cand_s)
        cand = jnp.where(cand_s < HI_NEG_INF, -jnp.inf, _hi16_as_float(cand_s)).astype(BF16)
        cnt = count_hi(cand)
        take = cnt >= kf
        return jnp.where(take, cand_u, t_hi), jnp.where(take, cnt, cnt_t)

    t_hi, cnt_t = lax.fori_loop(
        0, 16, hi_body, (jnp.zeros((1, tq), I32), jnp.zeros((1, tq), F32)))

    t_top = jnp.broadcast_to((t_hi << 16) ^ INT_MIN, (KEY_ACC_ROWS, tq))
    cnt_hi = cnt_t

    def byte_body(c, carry):
        for j in range(0, IDX_TK, KEY_ACC_ROWS):
            x = key_ref[c, j:j + KEY_ACC_ROWS, :] ^ t_top
            hi_ref[c, j:j + KEY_ACC_ROWS, :] = lax.shift_right_logical(x, 8).astype(F32).astype(BF16)
        return carry

    lax.fori_loop(0, nk, byte_body, 0)

    def mid_body(it, carry):
        t_mid, cnt_t = carry
        cand = t_mid | lax.shift_left(jnp.int32(1), 7 - it)
        cnt = cnt_hi - count_hi(cand.astype(F32).astype(BF16), below=True)
        take = cnt >= kf
        return jnp.where(take, cand, t_mid), jnp.where(take, cnt, cnt_t)

    t_mid, cnt_t = lax.fori_loop(0, 8, mid_body, (jnp.zeros((1, tq), I32), cnt_t))

    zero_bucket = t_hi == HI_OFFSET

    def unsettled(cnt):
        return jnp.max(jnp.where((cnt > kf) & jnp.logical_not(zero_bucket), 1.0, 0.0)) > 0.0

    def lo_step(it, t_u, cnt_t):
        cand_u = t_u | lax.shift_left(jnp.int32(1), 7 - it)
        cnt = count_key(lambda blk, cs: blk >= cs, cand_u ^ INT_MIN)
        take = cnt >= kf
        return jnp.where(take, cand_u, t_u), jnp.where(take, cnt, cnt_t)

    def lo_cond(carry):
        it, _, _, go = carry
        return jnp.logical_and(it < 8, go)

    def lo_body(carry):
        it, t_u, cnt_t, _ = carry
        t_u, cnt_t = lo_step(it, t_u, cnt_t)
        t_u, cnt_t = lo_step(it + 1, t_u, cnt_t)
        return it + 2, t_u, cnt_t, unsettled(cnt_t)

    _, t_u, cnt_t, _ = lax.while_loop(
        lo_cond, lo_body,
        (jnp.int32(0), (t_hi << 16) | (t_mid << 8), cnt_t, unsettled(cnt_t)))

    t_s = jnp.maximum(t_u ^ INT_MIN, INT_MIN + 1)
    tied = cnt_t > kf
    t_w = jnp.broadcast_to(t_s, (IDX_TK, tq))

    def emit(c, sel):
        k0 = pl.multiple_of(c * IDX_TK, IDX_TK)
        blk = jnp.where(sel, 0.0, -jnp.inf).astype(F32).T
        mask_ref[0, :, pl.ds(k0, IDX_TK)] = blk.astype(BF16)

    def write_plain():
        def body(c, carry):
            emit(c, key_ref[c] >= t_w)
            return carry
        lax.fori_loop(0, nk, body, 0)

    def write_tied():
        need = kf - count_key(lambda blk, ts: blk > ts, t_s)
        need_w = jnp.broadcast_to(need, (IDX_TK, tq))
        lower = jnp.where(lax.broadcasted_iota(I32, (TIE_ROWS, TIE_ROWS), 1) <=
                          lax.broadcasted_iota(I32, (TIE_ROWS, TIE_ROWS), 0), 1.0, 0.0).astype(BF16)

        def body(c, before):
            kc = key_ref[c]
            tie = kc == t_w
            tie01 = jnp.where(tie, 1.0, 0.0).astype(BF16)
            ranks = []
            for j in range(0, IDX_TK, TIE_ROWS):
                r = jnp.dot(lower, tie01[j:j + TIE_ROWS, :], preferred_element_type=F32) + before
                ranks.append(r)
                before = r[TIE_ROWS - 1:TIE_ROWS, :]
            rank = jnp.concatenate(ranks, axis=0)
            emit(c, (kc > t_w) | (tie & (rank <= need_w)))
            return before
        lax.fori_loop(0, nk, body, jnp.zeros((1, tq), F32))

    any_tied = jnp.max(jnp.where(tied, 1.0, 0.0)) > 0.0
    lax.cond(any_tied, write_tied, write_plain)

    def fill_body(c, carry):
        k0 = pl.multiple_of(c * IDX_TK, IDX_TK)
        mask_ref[0, :, pl.ds(k0, IDX_TK)] = jnp.full((tq, IDX_TK), -jnp.inf, BF16)
        return carry

    lax.fori_loop(nk, nkc, fill_body, 0)


def _index_mask(qi3, ki3, wt3, k_top):
    B, L, nqi = qi3.shape
    tq = IDX_TQ
    assert IDX_TK % tq == 0 and L % IDX_TK == 0
    assert (L // IDX_TK) * (IDX_TK // HI_ACC_ROWS) <= 256, "slot counts must stay exact in bf16"
    return pl.pallas_call(
        functools.partial(_index_kernel, k_top=k_top),
        grid=(B, L // tq),
        in_specs=[pl.BlockSpec((1, tq, nqi), lambda b, i: (b, i, 0)),
                  pl.BlockSpec((1, L, IDX_DIM), lambda b, i: (b, 0, 0)),
                  pl.BlockSpec((1, IDX_HEADS, tq), lambda b, i: (b, 0, i))],
        out_specs=pl.BlockSpec((1, tq, L), lambda b, i: (b, i, 0)),
        out_shape=jax.ShapeDtypeStruct((B, L, L), BF16),
        scratch_shapes=[pltpu.VMEM((L // IDX_TK, IDX_TK, tq), I32),
                        pltpu.VMEM((L // IDX_TK, IDX_TK, tq), BF16)],
        compiler_params=_params(("parallel", "parallel")),
        name="index_mask",
    )(qi3, ki3, wt3)


ATT_TQ = 512
ATT_TK = 1024


def _attn_kernel(q_ref, k_ref, v_ref, m_ref, o_ref, acc_ref, mx_ref):
    tq = q_ref.shape[1]
    tk = k_ref.shape[1]
    ngrp = tk // LANES
    i = pl.program_id(1)
    kb = pl.program_id(2)
    last = ((i + 1) * tq - 1) // tk

    @pl.when(kb == 0)
    def _():
        acc_ref[...] = jnp.zeros_like(acc_ref)
        mx_ref[...] = jnp.full_like(mx_ref, -jnp.inf)

    @pl.when(kb <= last)
    def _():
        bias = m_ref[0].astype(F32)
        ones = jnp.ones((tk, HEAD_DIM), BF16)
        for h in range(N_HEADS):
            hs = slice(h * HEAD_DIM, (h + 1) * HEAD_DIM)
            s = lax.dot_general(q_ref[0, :, hs], k_ref[0, :, hs], _NT,
                                preferred_element_type=F32) + bias
            m_old = mx_ref[h]
            m_new = jnp.maximum(m_old, jnp.max(s, axis=1, keepdims=True))
            m_safe = jnp.where(m_new == -jnp.inf, 0.0, m_new)
            alpha = jnp.exp2(m_old - m_safe)
            p = jnp.exp2(s - jnp.concatenate([m_safe] * ngrp, axis=1)).astype(BF16)
            v_ext = jnp.concatenate([v_ref[0, :, hs], ones], axis=1)
            acc_ref[h] = jnp.concatenate([alpha, alpha], axis=1) * acc_ref[h] + jnp.dot(
                p, v_ext, preferred_element_type=F32)
            mx_ref[h] = m_new

    @pl.when(kb == last)
    def _():
        for h in range(N_HEADS):
            hs = slice(h * HEAD_DIM, (h + 1) * HEAD_DIM)
            a = acc_ref[h]
            o_ref[0, :, hs] = (a[:, :HEAD_DIM] / a[:, HEAD_DIM:]).astype(BF16)


def _sparse_attention(qkv3, mask):
    B, L, _ = qkv3.shape
    W = N_HEADS * HEAD_DIM
    tq, tk = ATT_TQ, ATT_TK

    def kclamp(i, kb):
        return jnp.minimum(kb, ((i + 1) * tq - 1) // tk)

    return pl.pallas_call(
        _attn_kernel,
        grid=(B, L // tq, L // tk),
        in_specs=[pl.BlockSpec((1, tq, W), lambda b, i, kb: (b, i, 0)),
                  pl.BlockSpec((1, tk, W), lambda b, i, kb: (b, kclamp(i, kb), 1)),
                  pl.BlockSpec((1, tk, W), lambda b, i, kb: (b, kclamp(i, kb), 2)),
                  pl.BlockSpec((1, tq, tk), lambda b, i, kb: (b, i, kclamp(i, kb)))],
        out_specs=pl.BlockSpec((1, tq, W), lambda b, i, kb: (b, i, 0)),
        out_shape=jax.ShapeDtypeStruct((B, L, W), BF16),
        scratch_shapes=[pltpu.VMEM((N_HEADS, tq, 2 * HEAD_DIM), F32),
                        pltpu.VMEM((N_HEADS, tq, LANES), F32)],
        compiler_params=_params(("parallel", "parallel", "arbitrary")),
        name="sparse_attn",
    )(qkv3, qkv3, qkv3, mask)


def _mix_residual(x, attn, yc, ga, wao_ref, wmo_ref):
    ya = jnp.dot(attn, wao_ref[...], preferred_element_type=F32)
    m = yc.astype(F32) + ga.astype(F32) * ya
    return x + jnp.dot(m.astype(BF16), wmo_ref[...], preferred_element_type=F32)


def _memkv_kernel(mem_ref, g_ref, w_ref, o_ref):
    mn = _rms_bf16(mem_ref[0], g_ref[...])
    o_ref[0] = jnp.dot(mn, w_ref[...], preferred_element_type=F32).astype(BF16)


def _cross_attn_residual(h, g_ref, wq_ref, kv_ref, wo_ref):
    hn = _rms_bf16(h, g_ref[...])
    q = (jnp.dot(hn, wq_ref[...], preferred_element_type=F32) * (X_HEAD_DIM ** -0.5)).astype(BF16)
    outs = []
    for hh in range(X_HEADS):
        hs = slice(hh * X_HEAD_DIM, (hh + 1) * X_HEAD_DIM)
        vs = slice(X_W + hh * X_HEAD_DIM, X_W + (hh + 1) * X_HEAD_DIM)
        s = lax.dot_general(q[:, hs], kv_ref[0, :, hs], _NT, preferred_element_type=F32)
        p = jnp.exp(s - jnp.max(s, axis=1, keepdims=True))
        l = jnp.sum(p, axis=1, keepdims=True)
        o = jnp.dot(p.astype(BF16), kv_ref[0, :, vs], preferred_element_type=F32) / l
        outs.append(o.astype(BF16))
    o = jnp.concatenate(outs, axis=1)
    return h + jnp.dot(o, wo_ref[...], preferred_element_type=F32)


def _memory_kv(mem, g_mem, wkv):
    B, n_mem, D = mem.shape
    return pl.pallas_call(
        _memkv_kernel,
        grid=(B,),
        in_specs=[pl.BlockSpec((1, n_mem, D), lambda b: (b, 0, 0)),
                  pl.BlockSpec((1, D), lambda b: (0, 0)),
                  pl.BlockSpec((D, 2 * X_W), lambda b: (0, 0))],
        out_specs=pl.BlockSpec((1, n_mem, 2 * X_W), lambda b: (b, 0, 0)),
        out_shape=jax.ShapeDtypeStruct((B, n_mem, 2 * X_W), BF16),
        compiler_params=_params(("parallel",)),
        name="mem_kv",
    )(mem, g_mem, wkv)


def _merge_kernel(x_ref, attn_ref, yc_ref, ga_ref, wao_ref, wmo_ref, gx_ref, wq_ref, kv_ref,
                  wo_ref, o_ref):
    h = _mix_residual(x_ref[...], attn_ref[...], yc_ref[...], ga_ref[...], wao_ref, wmo_ref)
    o_ref[...] = _cross_attn_residual(h, gx_ref, wq_ref, kv_ref, wo_ref)


def _merge(x2, attn2, yc2, gates2, w_ao, w_mo, g_x, wq, kv, wo, tokens_per_batch, tm):
    T, D = x2.shape
    n_mem = kv.shape[1]
    tiles_per_batch = tokens_per_batch // tm
    row = lambda i: (i, 0)
    fixed = lambda i: (0, 0)
    return pl.pallas_call(
        _merge_kernel,
        grid=(T // tm,),
        in_specs=[pl.BlockSpec((tm, D), row),
                  pl.BlockSpec((tm, D), row),
                  pl.BlockSpec((tm, D), row),
                  pl.BlockSpec((tm, D), lambda i: (i, 1)),
                  pl.BlockSpec((D, D), fixed),
                  pl.BlockSpec((D, D), fixed),
                  pl.BlockSpec((1, D), fixed),
                  pl.BlockSpec((D, X_W), fixed),
                  pl.BlockSpec((1, n_mem, 2 * X_W), lambda i: (i // tiles_per_batch, 0, 0)),
                  pl.BlockSpec((X_W, D), fixed)],
        out_specs=pl.BlockSpec((tm, D), row),
        out_shape=jax.ShapeDtypeStruct((T, D), F32),
        compiler_params=_params(("parallel",)),
        name="merge_xattn",
    )(x2, attn2, yc2, gates2, w_ao, w_mo, g_x, wq, kv, wo)


def _ffn_kernel(h_ref, g_ref, w1_ref, w2_ref, gf_ref, o_ref, hn_ref, acc_ref):
    j = pl.program_id(1)

    @pl.when(j == 0)
    def _():
        h = h_ref[...]
        hn_ref[...] = _rms_bf16(h, g_ref[...])
        acc_ref[...] = h

    a = jnp.dot(hn_ref[...], w1_ref[...], preferred_element_type=F32)
    r = jnp.maximum(a, 0.0)
    acc_ref[...] += jnp.dot((r * r).astype(BF16), w2_ref[...], preferred_element_type=F32)

    @pl.when(j == pl.num_programs(1) - 1)
    def _():
        y = acc_ref[...]
        ms = jnp.mean(y * y, axis=-1, keepdims=True)
        o_ref[...] = y * lax.rsqrt(ms + EPS) * gf_ref[...]


def _ffn(h2, g, w1, w2, g_final, tm, tf):
    T, D = h2.shape
    F = w1.shape[1]
    row = lambda i, j: (i, 0)
    fixed = lambda i, j: (0, 0)
    return pl.pallas_call(
        _ffn_kernel,
        grid=(T // tm, F // tf),
        in_specs=[pl.BlockSpec((tm, D), row),
                  pl.BlockSpec((1, D), fixed),
                  pl.BlockSpec((D, tf), lambda i, j: (0, j)),
                  pl.BlockSpec((tf, D), lambda i, j: (j, 0)),
                  pl.BlockSpec((1, D), fixed)],
        out_specs=pl.BlockSpec((tm, D), row),
        out_shape=jax.ShapeDtypeStruct((T, D), F32),
        scratch_shapes=[pltpu.VMEM((tm, D), BF16), pltpu.VMEM((tm, D), F32)],
        compiler_params=_params(("parallel", "arbitrary")),
        name="ffn",
    )(h2, g, w1, w2, g_final)


def _layer(h3, mem, norm_mix_g, w_in, b_gate, conv_w, conv_b, conv_ln_g, conv_ln_b,
           w_conv_out, w_attn_out, w_mix_out, norm_x_g, norm_mem_g, wx_q, wx_kv, wx_o,
           norm_ffn_g, w_ff1, w_ff2, final_g):
    B, L, D = h3.shape
    T = B * L
    tm = TOKEN_TILE
    k_top = min(TOPK_MAX, L // 4)
    row = lambda v: v.reshape(1, -1)
    assert L % max(CONV_TILE, IDX_TQ, ATT_TQ, ATT_TK, FFN_TILE) == 0, "sequence must tile evenly"
    assert w_ff1.shape[1] % FFN_COLS == 0

    n_kw = IDX_DIM + IDX_HEADS
    w_all = jnp.concatenate(
        [w_in[:, :OFF_KW], jnp.pad(w_in[:, OFF_KW:OFF_KW + n_kw], ((0, 0), (0, LANES - n_kw))),
         w_in[:, OFF_KW + n_kw:]], axis=1).astype(BF16)
    assert w_all.shape[1] == PROJ_WIDTH

    x2 = h3.reshape(T, D)
    u, qkv, qi, ki, kw, gates = _projections(x2, row(norm_mix_g), w_all, row(b_gate), tm)

    yc = _conv_branch(u.reshape(B, L, D), gates.reshape(B, L, 2 * D), conv_w, row(conv_b),
                      row(conv_ln_g), row(conv_ln_b), w_conv_out.astype(BF16), CONV_TILE)

    wt = kw.reshape(B, L, LANES)[:, :, IDX_DIM:IDX_DIM + IDX_HEADS].transpose(0, 2, 1)
    mask = _index_mask(qi.reshape(B, L, -1), ki.reshape(B, L, IDX_DIM), wt, k_top)
    attn = _sparse_attention(qkv.reshape(B, L, 3 * D), mask)

    kv = _memory_kv(mem, row(norm_mem_g), wx_kv.astype(BF16))
    h2 = _merge(x2, attn.reshape(T, D), yc.reshape(T, D), gates,
                w_attn_out.astype(BF16), w_mix_out.astype(BF16),
                row(norm_x_g), wx_q.astype(BF16), kv, wx_o.astype(BF16), L, tm)
    out = _ffn(h2, row(norm_ffn_g), w_ff1.astype(BF16), w_ff2.astype(BF16), final_g,
               FFN_TILE, FFN_COLS)
    return out.reshape(B, L, D)


def kernel(x, mem, norm_mix_g, w_in, b_gate, conv_w, conv_b, conv_ln_g, conv_ln_b, w_conv_out,
           w_attn_out, w_mix_out, norm_x_g, norm_mem_g, wx_q, wx_kv, wx_o, norm_ffn_g, w_ff1,
           w_ff2, norm_final_g):
    depth = w_in.shape[0]
    assert depth == 1, "final RMSNorm is fused into the last layer's MLP kernel"
    assert x.shape[-1] == D_MODEL
    return _layer(x, mem, norm_mix_g[0], w_in[0], b_gate[0], conv_w[0], conv_b[0], conv_ln_g[0],
                  conv_ln_b[0], w_conv_out[0], w_attn_out[0], w_mix_out[0], norm_x_g[0],
                  norm_mem_g[0], wx_q[0], wx_kv[0], wx_o[0], norm_ffn_g[0], w_ff1[0], w_ff2[0],
                  norm_final_g.reshape(1, -1))
```
